```python
import math
import jax
import jax.numpy as jnp
from jax import lax
import numpy as np

D_MODEL = 1024
BATCH = 8
SEQ = 2048
DEPTH = 4
DEC_BATCH = 32
DEC_SEQ = 8
PAST_LEN = 16384
PAGE_SIZE = 128

HEAD_DIM = 64
A_HEADS = 4
A_KV_HEADS = 2
B_HEADS = 4
B_KV_HEADS = 2
MOBA_BLOCK = 256
MOBA_TOPK = 3
MOBA_Q_BLOCK = 32
C_HEADS = 4
C_QK_DIM = HEAD_DIM // 2
C_V_DIM = HEAD_DIM
D_HEADS = 4
D_NOPE = 64
D_ROPE = 32
D_V = 64
D_Q_LORA = D_MODEL // 4
D_KV_LORA = D_MODEL // 8
ROPE_BASE = 10000.0
N_BRANCH = 4
BRANCH_W = A_HEADS * HEAD_DIM
N_BUCKETS = 32
MAX_DISTANCE = 128
Q_BLOCK = 128
FORGET_BIAS_INIT = 2.0
EPS = 1e-6
NEG_INF = -1e30

IN_SPLITS = (
    A_HEADS * HEAD_DIM, A_KV_HEADS * HEAD_DIM, A_KV_HEADS * HEAD_DIM, A_HEADS, BRANCH_W,
    B_HEADS * HEAD_DIM, B_KV_HEADS * HEAD_DIM, B_KV_HEADS * HEAD_DIM, BRANCH_W,
    C_HEADS * 2 * C_QK_DIM, C_HEADS * 2 * C_QK_DIM, C_HEADS * C_V_DIM, BRANCH_W,
    D_Q_LORA, D_KV_LORA, D_ROPE, BRANCH_W,
    N_BRANCH * D_MODEL,
)
N_IN = sum(IN_SPLITS)

kernel_name = 'hybrid_fox_moba_diff_mla_step'


def rmsnorm(x, g):
    xf = x.astype(jnp.float32)
    y = xf * lax.rsqrt(jnp.mean(xf * xf, axis=-1, keepdims=True) + EPS)
    return (y * g.astype(jnp.float32)).astype(x.dtype)


def masked_softmax(s, mask):
    return jax.nn.softmax(jnp.where(mask, s, NEG_INF), axis=-1)


def t5_bucket(dist):
    max_exact = N_BUCKETS // 2
    d = jnp.maximum(dist, 0)
    df = jnp.maximum(d, 1).astype(jnp.float32)
    large = max_exact + (jnp.log(df / max_exact) / math.log(MAX_DISTANCE / max_exact)
                         * (N_BUCKETS - max_exact)).astype(jnp.int32)
    large = jnp.minimum(large, N_BUCKETS - 1)
    return jnp.where(d < max_exact, d, large)


def rope(x, pos):
    half = x.shape[-1] // 2
    inv = jnp.power(ROPE_BASE, -jnp.arange(half, dtype=jnp.float32) / half)
    ang = pos.astype(jnp.float32)[:, None] * inv
    ang = ang.reshape((1, ang.shape[0]) + (1,) * (x.ndim - 3) + (half,))
    c, s = jnp.cos(ang), jnp.sin(ang)
    xf = x.astype(jnp.float32)
    x1, x2 = xf[..., :half], xf[..., half:]
    return jnp.concatenate([x1 * c - x2 * s, x1 * s + x2 * c], axis=-1).astype(x.dtype)


def sweep_queries(fn, n_q, qb):
    out = lax.map(fn, jnp.arange(n_q // qb, dtype=jnp.int32) * qb)
    out = jnp.moveaxis(out, 0, 1)
    return out.reshape((out.shape[0], n_q) + out.shape[3:])


def fox_attention(q, k, v, F, q0):
    B, T, H, Dh = q.shape
    L, KVH = k.shape[1], k.shape[2]
    G = H // KVH
    qg = q.reshape(B, T, KVH, G, Dh)
    Fq = F[:, L - T:].reshape(B, T, KVH, G)
    Fk = jnp.transpose(F.reshape(B, L, KVH, G), (0, 2, 3, 1))[:, :, :, None, :]
    kpos = jnp.arange(L)
    qb = math.gcd(T, Q_BLOCK)
    scale = Dh ** -0.5

    def block(start):
        qs = lax.dynamic_slice_in_dim(qg, start, qb, axis=1)
        fq = jnp.transpose(lax.dynamic_slice_in_dim(Fq, start, qb, axis=1), (0, 2, 3, 1))[..., None]
        qpos = q0 + start + jnp.arange(qb)
        s = jnp.einsum('btkgd,bskd->bkgts', qs, k, preferred_element_type=jnp.float32) * scale + (fq - Fk)
        p = masked_softmax(s, kpos[None, :] <= qpos[:, None])
        return jnp.einsum('bkgts,bskd->btkgd', p.astype(v.dtype), v)

    return sweep_queries(block, T, qb).reshape(B, T, H * Dh)


def moba_attention(q, k, v, bias_tab, q0):
    B, T, H, Dh = q.shape
    L, KVH = k.shape[1], k.shape[2]
    G = H // KVH
    nb = -(-L // MOBA_BLOCK)
    pad = nb * MOBA_BLOCK - L
    kb = jnp.pad(k, ((0, 0), (0, pad), (0, 0), (0, 0))).reshape(B, nb, MOBA_BLOCK, KVH, Dh)
    vb = jnp.pad(v, ((0, 0), (0, pad), (0, 0), (0, 0))).reshape(B, nb, MOBA_BLOCK, KVH, Dh)
    k_mean = jnp.mean(kb.astype(jnp.float32), axis=2)
    k_src = jnp.transpose(kb, (0, 3, 1, 2, 4)).reshape(B, KVH, nb, MOBA_BLOCK * Dh)
    v_src = jnp.transpose(vb, (0, 3, 1, 2, 4)).reshape(B, KVH, nb, MOBA_BLOCK * Dh)
    qg = q.reshape(B, T, KVH, G, Dh)
    bias_kg = jnp.transpose(bias_tab).reshape(KVH, G, N_BUCKETS).astype(jnp.float32)
    kv_i = jnp.arange(KVH).reshape(1, KVH, 1, 1, 1, 1)
    g_i = jnp.arange(G).reshape(1, 1, G, 1, 1, 1)
    n_sel = min(MOBA_TOPK, nb)
    ns = n_sel + 1
    qb = math.gcd(T, MOBA_Q_BLOCK)
    blk_ids = jnp.arange(nb)
    offs = jnp.arange(MOBA_BLOCK)
    scale = Dh ** -0.5

    def block(start):
        qs = lax.dynamic_slice_in_dim(qg, start, qb, axis=1)
        qpos = q0 + start + jnp.arange(qb)
        own = qpos // MOBA_BLOCK
        gate = jnp.einsum('btkgd,bnkd->bkgtn', qs.astype(jnp.float32), k_mean)
        gate = jnp.where(blk_ids[None, :] < own[:, None], gate, NEG_INF)
        _, top = lax.top_k(gate, n_sel)
        own5 = own.reshape(1, 1, 1, qb, 1)
        sel_ok = top < own5
        idx = jnp.concatenate([top, jnp.broadcast_to(own5, top.shape[:-1] + (1,)).astype(top.dtype)], axis=-1)
        flat = idx.reshape(B, KVH, G * qb * ns)[..., None]
        kg = jnp.take_along_axis(k_src, flat, axis=2).reshape(B, KVH, G, qb, ns, MOBA_BLOCK, Dh)
        vg = jnp.take_along_axis(v_src, flat, axis=2).reshape(B, KVH, G, qb, ns, MOBA_BLOCK, Dh)
        kpos = idx[..., None] * MOBA_BLOCK + offs
        dist = qpos.reshape(1, 1, 1, qb, 1, 1) - kpos
        bias = bias_kg[kv_i, g_i, t5_bucket(dist)]
        s = jnp.einsum('btkgd,bkgtjsd->bkgtjs', qs, kg, preferred_element_type=jnp.float32) * scale + bias
        valid = jnp.concatenate([sel_ok, jnp.ones_like(sel_ok[..., :1])], axis=-1)[..., None] & (dist >= 0)
        p = masked_softmax(s.reshape(B, KVH, G, qb, ns * MOBA_BLOCK),
                           valid.reshape(B, KVH, G, qb, ns * MOBA_BLOCK))
        p = p.reshape(B, KVH, G, qb, ns, MOBA_BLOCK).astype(vg.dtype)
        return jnp.einsum('bkgtjs,bkgtjsd->btkgd', p, vg)

    return sweep_queries(block, T, qb).reshape(B, T, H * Dh)


def diff_attention(q, k, v, lam, bias_tab, q0):
    B, T, H = q.shape[:3]
    L = k.shape[1]
    kpos = jnp.arange(L)
    qb = math.gcd(T, Q_BLOCK)
    scale = C_QK_DIM ** -0.5

    def block(start):
        qs = lax.dynamic_slice_in_dim(q, start, qb, axis=1)
        qpos = q0 + start + jnp.arange(qb)
        dist = qpos[:, None] - kpos[None, :]
        bias = jnp.transpose(bias_tab[t5_bucket(dist)], (2, 0, 1)).astype(jnp.float32)
        s = jnp.einsum('btham,bsham->bhats', qs, k, preferred_element_type=jnp.float32) * scale + bias[None, :, None]
        p = masked_softmax(s, dist >= 0)
        a = p[:, :, 0] - lam * p[:, :, 1]
        return jnp.einsum('bhts,bshd->bthd', a.astype(v.dtype), v)

    return sweep_queries(block, T, qb)


def mla_attention(q_nope, q_rope, ckv, krope, w_uk, w_uv, q0):
    B, T, H, _ = q_nope.shape
    L = ckv.shape[1]
    q_lat = jnp.einsum('bthn,chn->bthc', q_nope, w_uk)
    kpos = jnp.arange(L)
    qb = math.gcd(T, Q_BLOCK)
    scale = (D_NOPE + D_ROPE) ** -0.5

    def block(start):
        ql = lax.dynamic_slice_in_dim(q_lat, start, qb, axis=1)
        qr = lax.dynamic_slice_in_dim(q_rope, start, qb, axis=1)
        qpos = q0 + start + jnp.arange(qb)
        s = (jnp.einsum('bthc,bsc->bhts', ql, ckv, preferred_element_type=jnp.float32)
             + jnp.einsum('bthr,bsr->bhts', qr, krope, preferred_element_type=jnp.float32)) * scale
        p = masked_softmax(s, kpos[None, :] <= qpos[:, None])
        return jnp.einsum('bhts,bsc->bthc', p.astype(ckv.dtype), ckv)

    o_lat = sweep_queries(block, T, qb)
    return jnp.einsum('bthc,chv->bthv', o_lat, w_uv).reshape(B, T, H * D_V)


def hybrid_layer(x, past, q0, lam_init, norm_g, w_in, b_forget, b_gate, d_q_norm_g, d_w_q_up,
                 d_kv_norm_g, d_w_kv_up, c_lambda, c_subln_g, w_branch, w_out, rel_bias):
    B, T, _ = x.shape
    h = rmsnorm(x, norm_g)
    proj = jnp.einsum('btd,dn->btn', h, w_in)
    (a_q, a_k, a_v, a_f, a_z, b_q, b_k, b_v, b_z, c_q, c_k, c_v, c_z,
     d_qa, d_kva, d_kr, d_z, gates) = jnp.split(proj, np.cumsum(IN_SPLITS)[:-1].tolist(), axis=-1)
    pos = q0 + jnp.arange(T)

    a_k = a_k.reshape(B, T, A_KV_HEADS, HEAD_DIM)
    a_v = a_v.reshape(B, T, A_KV_HEADS, HEAD_DIM)
    a_logf = jax.nn.log_sigmoid(a_f.astype(jnp.float32) + b_forget.astype(jnp.float32)).astype(x.dtype)
    b_k = b_k.reshape(B, T, B_KV_HEADS, HEAD_DIM)
    b_v = b_v.reshape(B, T, B_KV_HEADS, HEAD_DIM)
    c_k = c_k.reshape(B, T, C_HEADS, 2 * C_QK_DIM)
    c_v = c_v.reshape(B, T, C_HEADS, C_V_DIM)
    d_ckv = rmsnorm(d_kva, d_kv_norm_g)
    d_kr = rope(d_kr, pos)
    rows = (a_k, a_v, a_logf, b_k, b_v, c_k, c_v, d_ckv, d_kr)
    if past is None:
        full = rows
    else:
        full = tuple(jnp.concatenate([p.astype(r.dtype), r], axis=1) for p, r in zip(past, rows))
    fa_k, fa_v, fa_logf, fb_k, fb_v, fc_k, fc_v, fd_ckv, fd_kr = full
    L = fa_k.shape[1]

    F = jnp.cumsum(fa_logf.astype(jnp.float32), axis=1)
    o_a = fox_attention(a_q.reshape(B, T, A_HEADS, HEAD_DIM), fa_k, fa_v, F, q0)
    o_b = moba_attention(b_q.reshape(B, T, B_HEADS, HEAD_DIM), fb_k, fb_v, rel_bias[:, :B_HEADS], q0)
    cl = c_lambda.astype(jnp.float32)
    lam = jnp.exp(jnp.sum(cl[0] * cl[1])) - jnp.exp(jnp.sum(cl[2] * cl[3])) + lam_init
    o_c = diff_attention(c_q.reshape(B, T, C_HEADS, 2, C_QK_DIM), fc_k.reshape(B, L, C_HEADS, 2, C_QK_DIM),
                         fc_v, lam, rel_bias[:, B_HEADS:], q0)
    o_c = (rmsnorm(o_c, c_subln_g) * (1.0 - lam_init)).reshape(B, T, BRANCH_W)
    q = jnp.einsum('btr,rhe->bthe', rmsnorm(d_qa, d_q_norm_g), d_w_q_up)
    q_nope, q_rope = q[..., :D_NOPE], rope(q[..., D_NOPE:], pos)
    o_d = mla_attention(q_nope, q_rope, fd_ckv, fd_kr, d_w_kv_up[..., :D_NOPE], d_w_kv_up[..., D_NOPE:], q0)

    branches = jnp.stack([o_a, o_b, o_c, o_d], axis=2)
    z = jnp.stack([a_z, b_z, c_z, d_z], axis=2)
    u = jnp.einsum('btnw,nwd->btnd', branches * jax.nn.silu(z), w_branch)
    g = jax.nn.sigmoid((gates + b_gate).reshape(B, T, N_BRANCH, D_MODEL))
    y = jnp.einsum('btd,de->bte', jnp.sum(g * u, axis=2), w_out)
    return x + y, rows


def setup_inputs(seed: int = 0) -> dict:
    key = jax.random.key(seed)
    ks = jax.random.split(key, 28)
    f32 = jnp.float32
    n_pages = PAST_LEN // PAGE_SIZE
    n_phys = (5 * DEC_BATCH * n_pages) // 4
    pool = (DEPTH, n_phys, PAGE_SIZE)

    def nrm(k, shape, scale=1.0):
        return jax.random.normal(k, shape, f32) * scale

    def gain(k, shape):
        return 1.0 + 0.02 * jax.random.normal(k, shape, f32)

    perm = jax.random.permutation(ks[11], n_phys)
    page_table = perm[:DEC_BATCH * n_pages].reshape(DEC_BATCH, n_pages).astype(jnp.int32)
    return {
        'x_prompt': nrm(ks[0], (BATCH, SEQ, D_MODEL)),
        'x_sample': nrm(ks[1], (DEC_BATCH, DEC_SEQ, D_MODEL)),
        'cache_a_k': nrm(ks[2], pool + (A_KV_HEADS, HEAD_DIM)),
        'cache_a_v': nrm(ks[3], pool + (A_KV_HEADS, HEAD_DIM)),
        'cache_a_logf': jax.nn.log_sigmoid(nrm(ks[4], pool + (A_HEADS,)) + FORGET_BIAS_INIT),
        'cache_b_k': nrm(ks[5], pool + (B_KV_HEADS, HEAD_DIM)),
        'cache_b_v': nrm(ks[6], pool + (B_KV_HEADS, HEAD_DIM)),
        'cache_c_k': nrm(ks[7], pool + (C_HEADS, 2 * C_QK_DIM)),
        'cache_c_v': nrm(ks[8], pool + (C_HEADS, C_V_DIM)),
        'cache_d_ckv': nrm(ks[9], pool + (D_KV_LORA,)),
        'cache_d_kr': nrm(ks[10], pool + (D_ROPE,)),
        'page_table': page_table,
        'norm_g': gain(ks[12], (DEPTH, D_MODEL)),
        'w_in': nrm(ks[13], (DEPTH, D_MODEL, N_IN), D_MODEL ** -0.5),
        'b_forget': FORGET_BIAS_INIT + nrm(ks[14], (DEPTH, A_HEADS), 0.1),
        'b_gate': nrm(ks[15], (DEPTH, N_BRANCH * D_MODEL), 0.02),
        'd_q_norm_g': gain(ks[16], (DEPTH, D_Q_LORA)),
        'd_w_q_up': nrm(ks[17], (DEPTH, D_Q_LORA, D_HEADS, D_NOPE + D_ROPE), D_Q_LORA ** -0.5),
        'd_kv_norm_g': gain(ks[18], (DEPTH, D_KV_LORA)),
        'd_w_kv_up': nrm(ks[19], (DEPTH, D_KV_LORA, D_HEADS, D_NOPE + D_V), D_KV_LORA ** -0.5),
        'c_lambda': nrm(ks[20], (DEPTH, 4, C_QK_DIM), 0.1),
        'c_subln_g': gain(ks[21], (DEPTH, C_V_DIM)),
        'w_branch': nrm(ks[22], (DEPTH, N_BRANCH, BRANCH_W, D_MODEL), BRANCH_W ** -0.5),
        'w_out': nrm(ks[23], (DEPTH, D_MODEL, D_MODEL), D_MODEL ** -0.5),
        'rel_bias': nrm(ks[24], (N_BUCKETS, B_HEADS + C_HEADS), 0.5),
        'final_norm_g': gain(ks[25], (D_MODEL,)),
    }


def reference(x_prompt, x_sample, cache_a_k, cache_a_v, cache_a_logf, cache_b_k, cache_b_v,
              cache_c_k, cache_c_v, cache_d_ckv, cache_d_kr, page_table, norm_g, w_in, b_forget,
              b_gate, d_q_norm_g, d_w_q_up, d_kv_norm_g, d_w_kv_up, c_lambda, c_subln_g, w_branch,
              w_out, rel_bias, final_norm_g):
    q0_sample = page_table.shape[1] * cache_a_k.shape[2]

    def gather_pages(c):
        g = c[page_table]
        return g.reshape((g.shape[0], g.shape[1] * g.shape[2]) + g.shape[3:])

    def stack(rows, i):
        return jnp.stack([r[i] for r in rows], axis=0)

    h_p, h_s = x_prompt, x_sample
    rows_p, rows_s = [], []
    for l in range(DEPTH):
        lam_init = 0.8 - 0.6 * math.exp(-0.3 * l)
        weights = (norm_g[l], w_in[l], b_forget[l], b_gate[l], d_q_norm_g[l], d_w_q_up[l],
                   d_kv_norm_g[l], d_w_kv_up[l], c_lambda[l], c_subln_g[l], w_branch[l], w_out[l], rel_bias)
        past = (gather_pages(cache_a_k[l]), gather_pages(cache_a_v[l]), gather_pages(cache_a_logf[l]),
                gather_pages(cache_b_k[l]), gather_pages(cache_b_v[l]),
                gather_pages(cache_c_k[l]), gather_pages(cache_c_v[l]),
                gather_pages(cache_d_ckv[l]), gather_pages(cache_d_kr[l]))
        h_p, r_p = hybrid_layer(h_p, None, 0, lam_init, *weights)
        h_s, r_s = hybrid_layer(h_s, past, q0_sample, lam_init, *weights)
        rows_p.append(r_p)
        rows_s.append(r_s)

    y_prompt = rmsnorm(h_p, final_norm_g)
    y_sample = rmsnorm(h_s, final_norm_g)
    return (y_prompt, y_sample,
            stack(rows_p, 0), stack(rows_p, 1), stack(rows_p, 2), stack(rows_p, 3), stack(rows_p, 4),
            stack(rows_p, 5), stack(rows_p, 6), stack(rows_p, 7), stack(rows_p, 8),
            stack(rows_s, 0), stack(rows_s, 1), stack(rows_s, 2), stack(rows_s, 3), stack(rows_s, 4),
            stack(rows_s, 5), stack(rows_s, 6), stack(rows_s, 7), stack(rows_s, 8))
```

```python
import functools
import math

import jax
import jax.numpy as jnp
import numpy as np
from jax import lax
from jax.experimental import pallas as pl
from jax.experimental.pallas import tpu as pltpu

F32 = jnp.float32
BF16 = jnp.bfloat16

HEAD_DIM = 64
N_HEADS = 4
KV_HEADS = 2
MOBA_BLOCK = 256
MOBA_TOPK = 3
C_QK_DIM = 32
D_NOPE = 64
D_ROPE = 32
D_V = 64
D_LAT = 128
D_QLORA = 256
ROPE_BASE = 10000.0
N_BUCKETS = 32
MAX_DISTANCE = 128
EPS = 1e-6
NEG_INF = -1e30

LANES = 128
TQ = 256
PAGES_PER_STEP = 8
VMEM_LIMIT = 48 * 1024 * 1024

HEAD_PERM = (0, 2, 1, 3)

_SEGS = (("qa", 256), ("ka", 128), ("va", 128), ("qb", 256), ("kb", 128), ("vb", 128),
         ("qc", 256), ("kc", 256), ("vc", 256), ("dqa", 256), ("dkva", 128),
         ("kr", 128), ("krs", 128), ("af", 128), ("z", 1024))
_OFF = {}
_o = 0
for _n, _w in _SEGS:
    _OFF[_n] = (_o, _o + _w)
    _o += _w
N_PACK = _o


def _cparams(sem):
    return pltpu.CompilerParams(dimension_semantics=sem, vmem_limit_bytes=VMEM_LIMIT)


def _dot(a, b):
    return jnp.dot(a, b, preferred_element_type=F32)


def _dot_nt(a, b, precision=None):
    return lax.dot_general(a, b, (((1,), (1,)), ((), ())), preferred_element_type=F32,
                           precision=precision)


def _rms(x, g):
    return x * lax.rsqrt(jnp.mean(x * x, axis=-1, keepdims=True) + EPS) * g


def _inproj_kernel(x_ref, g_ref, w_ref, bf_ref, gq_ref, gkv_ref, wq_ref, wuk_ref, cos_ref, sin_ref,
                   qa_ref, ka_ref, va_ref, lf_ref, qb_ref, qbf_ref, kb_ref, vb_ref,
                   qc_ref, kc_ref, vc_ref, qd_ref, ckv_ref, kr_ref, z_ref):
    hb = _rms(x_ref[...], g_ref[...]).astype(BF16)
    proj = _dot(hb, w_ref[...])

    def seg(name):
        lo, hi = _OFF[name]
        return proj[:, lo:hi]

    qa_ref[...] = seg("qa").astype(BF16)
    ka_ref[...] = seg("ka")
    va_ref[...] = seg("va")
    af = seg("af") + bf_ref[...]
    lf_ref[...] = jnp.minimum(af, 0.0) - jnp.log(1.0 + jnp.exp(-jnp.abs(af)))
    qb = seg("qb")
    qb_ref[...] = qb.astype(BF16)
    qbf_ref[...] = qb
    kb_ref[...] = seg("kb")
    vb_ref[...] = seg("vb")
    qc_ref[...] = seg("qc").astype(BF16)
    kc_ref[...] = seg("kc")
    vc_ref[...] = seg("vc")
    z_ref[...] = seg("z").astype(BF16)
    ckv_ref[...] = _rms(seg("dkva"), gkv_ref[...])
    cos = cos_ref[...]
    sin = sin_ref[...]
    kr_ref[...] = seg("kr") * cos + seg("krs") * sin
    qn = _rms(seg("dqa"), gq_ref[...]).astype(BF16)
    qq = _dot(qn, wq_ref[...])
    q_lat = _dot(qq[:, :256].astype(BF16), wuk_ref[...])
    cos8 = jnp.concatenate([cos] * 8, axis=1)
    sin8 = jnp.concatenate([sin] * 8, axis=1)
    q_rope = qq[:, 256:1280] * cos8 + qq[:, 1280:2304] * sin8
    qd_ref[...] = (q_lat + q_rope).astype(BF16)


def _inproj(x, lw, cos_t, sin_t, tm):
    m = x.shape[0]
    n_t = cos_t.shape[0] // tm
    grid = (m // tm,)
    row = lambda w: pl.BlockSpec((tm, w), lambda i: (i, 0))
    full = lambda a: pl.BlockSpec(a.shape, lambda i: (0,) * a.ndim)
    tab = pl.BlockSpec((tm, LANES), lambda i: (i % n_t, 0))
    outs = (("qa", 256, BF16), ("ka", 128, F32), ("va", 128, F32), ("lf", 128, F32),
            ("qb", 256, BF16), ("qbf", 256, F32), ("kb", 128, F32), ("vb", 128, F32),
            ("qc", 256, BF16), ("kc", 256, F32), ("vc", 256, F32), ("qd", 1024, BF16),
            ("ckv", 128, F32), ("kr", 128, F32), ("z", 1024, BF16))
    res = pl.pallas_call(
        _inproj_kernel,
        grid=grid,
        in_specs=[row(x.shape[1]), full(lw["norm_g"]), full(lw["w_pack"]), full(lw["b_forget"]),
                  full(lw["gq"]), full(lw["gkv"]), full(lw["wq"]), full(lw["wuk"]), tab, tab],
        out_specs=[row(w) for _, w, _ in outs],
        out_shape=[jax.ShapeDtypeStruct((m, w), dt) for _, w, dt in outs],
        compiler_params=_cparams(("arbitrary",)),
        name="inproj",
    )(x, lw["norm_g"], lw["w_pack"], lw["b_forget"], lw["gq"], lw["gkv"], lw["wq"], lw["wuk"],
      cos_t, sin_t)
    return {n: r for (n, _, _), r in zip(outs, res)}


def _merge_kernel(x_ref, oa_ref, ob_ref, oc_ref, od_ref, z_ref, g_ref, wg_ref, bg_ref, wbr_ref,
                  wout_ref, fg_ref, y_ref, *, final):
    x = x_ref[...]
    hb = _rms(x, g_ref[...]).astype(BF16)
    d = x.shape[1]
    acc = jnp.zeros(x.shape, F32)
    for n, o_ref in enumerate((oa_ref, ob_ref, oc_ref, od_ref)):
        w = o_ref.shape[1]
        z = z_ref[:, n * w:(n + 1) * w].astype(F32)
        a = (o_ref[...].astype(F32) * (z * jax.nn.sigmoid(z))).astype(BF16)
        u = _dot(a, wbr_ref[n])
        gate = _dot(hb, wg_ref[:, n * d:(n + 1) * d]) + bg_ref[:, n * d:(n + 1) * d]
        acc = acc + jax.nn.sigmoid(gate) * u
    y = x + _dot(acc.astype(BF16), wout_ref[...])
    if final:
        y = _rms(y, fg_ref[...])
    y_ref[...] = y


def _merge(x, o_a, o_b, o_c, o_d, z, lw, final_g, final, tm):
    m, d = x.shape
    row = lambda w: pl.BlockSpec((tm, w), lambda i: (i, 0))
    full = lambda a: pl.BlockSpec(a.shape, lambda i: (0,) * a.ndim)
    return pl.pallas_call(
        functools.partial(_merge_kernel, final=final),
        grid=(m // tm,),
        in_specs=[row(d), row(256), row(256), row(256), row(256), row(1024), full(lw["norm_g"]),
                  full(lw["wg"]), full(lw["b_gate"]), full(lw["wbr"]), full(lw["wout"]),
                  full(final_g)],
        out_specs=row(d),
        out_shape=jax.ShapeDtypeStruct((m, d), F32),
        compiler_params=_cparams(("arbitrary",)),
        name="merge",
    )(x, o_a, o_b, o_c, o_d, z, lw["norm_g"], lw["wg"], lw["b_gate"], lw["wbr"], lw["wout"],
      final_g)


def _lane_lo(shape):
    return lax.broadcasted_iota(jnp.int32, shape, len(shape) - 1) < (LANES // 2)


def _blockdiag_rows(q, n_groups):
    lane = lax.broadcasted_iota(jnp.int32, q.shape, 1)
    gw = LANES // n_groups
    zero = jnp.zeros_like(q)
    return jnp.concatenate(
        [jnp.where((lane >= g * gw) & (lane < (g + 1) * gw), q, zero) for g in range(n_groups)], axis=0)


def _softmax_step(s_blocks, m, l, acc, v):
    t = s_blocks[0].shape[0]
    p_blocks, m_new_blocks, alpha_blocks, l_blocks = [], [], [], []
    for r, s in enumerate(s_blocks):
        m_old = m[r * t:(r + 1) * t]
        m_new = jnp.maximum(m_old, jnp.max(s, axis=-1, keepdims=True))
        p = jnp.exp(s - m_new)
        alpha = jnp.exp(m_old - m_new)
        l_blocks.append(alpha * l[r * t:(r + 1) * t] + jnp.sum(p, axis=-1, keepdims=True))
        p_blocks.append(p.astype(BF16))
        m_new_blocks.append(m_new)
        alpha_blocks.append(alpha)
    p_all = jnp.concatenate(p_blocks, axis=0)
    alpha_all = jnp.concatenate(alpha_blocks, axis=0)
    acc = alpha_all * acc + _dot(p_all, v)
    return jnp.concatenate(m_new_blocks, axis=0), jnp.concatenate(l_blocks, axis=0), acc


def _init_state(rows):
    return (jnp.full((rows, 1), -jnp.inf, F32), jnp.zeros((rows, 1), F32), jnp.zeros((rows, LANES), F32))


def _subln(pair, g2, lam_init):
    lo = _lane_lo(pair.shape)
    sq = pair * pair
    ss_lo = jnp.sum(jnp.where(lo, sq, 0.0), axis=-1, keepdims=True)
    ss_hi = jnp.sum(jnp.where(lo, 0.0, sq), axis=-1, keepdims=True)
    ms = jnp.where(lo, ss_lo, ss_hi) * (1.0 / C_V_HALF)
    return (pair * lax.rsqrt(ms + EPS) * g2) * (1.0 - lam_init)


C_V_HALF = 64.0


def _diff_lambda(cl, lam_init):
    a = jnp.sum(cl[0:1] * cl[1:2], axis=-1, keepdims=True)
    b = jnp.sum(cl[2:3] * cl[3:4], axis=-1, keepdims=True)
    return jnp.exp(a) - jnp.exp(b) + lam_init


def _cumsum_kernel(x_ref, f_ref):
    x = x_ref[0]
    t = x.shape[1]
    lane = lax.broadcasted_iota(jnp.int32, x.shape, 1) % LANES
    s = 1
    while s < LANES:
        x = x + jnp.where(lane >= s, pltpu.roll(x, s, 1), 0.0)
        s *= 2
    carry = jnp.zeros((x.shape[0], 1), F32)
    for c in range(t // LANES):
        blk = x[:, c * LANES:(c + 1) * LANES] + carry
        f_ref[0, :, c * LANES:(c + 1) * LANES] = blk
        carry = blk[:, LANES - 1:LANES]


def _cumsum_rows(lf_t):
    b, h, t = lf_t.shape
    spec = pl.BlockSpec((1, h, t), lambda i: (i, 0, 0))
    return pl.pallas_call(
        _cumsum_kernel, grid=(b,), in_specs=[spec], out_specs=spec,
        out_shape=jax.ShapeDtypeStruct(lf_t.shape, F32),
        compiler_params=_cparams(("arbitrary",)), name="fox_cumsum",
    )(lf_t)


def _fox_prompt_kernel(q_ref, k_ref, v_ref, fcol_ref, frow_ref, mask_ref, o_ref, kb_ref, vb_ref):
    i = pl.program_id(1)

    @pl.when(i == 0)
    def _():
        kb_ref[...] = k_ref[0].astype(BF16)
        vb_ref[...] = v_ref[0].astype(BF16)

    q = q_ref[0]
    qbd = jnp.concatenate([_blockdiag_rows(q[:, :LANES], 2), _blockdiag_rows(q[:, LANES:], 2)], axis=0)
    fcol = fcol_ref[0]
    fq = [fcol[:, h:h + 1] for h in HEAD_PERM]

    def body(j, carry):
        m, l, acc = carry
        off = pl.multiple_of(j * TQ, TQ)
        s = _dot_nt(qbd, kb_ref[pl.ds(off, TQ), :])
        mask = mask_ref[jnp.minimum(i - j, 1)]
        blocks = []
        for r, h in enumerate(HEAD_PERM):
            fk = frow_ref[0, h:h + 1, pl.ds(off, TQ)]
            blocks.append(s[r * TQ:(r + 1) * TQ] + (fq[r] - fk) + mask)
        return _softmax_step(blocks, m, l, acc, vb_ref[pl.ds(off, TQ), :])

    m, l, acc = lax.fori_loop(0, i + 1, body, _init_state(4 * TQ))
    o = acc / l
    lo = _lane_lo((TQ, LANES))
    o_ref[0, :, :LANES] = jnp.where(lo, o[:TQ], o[TQ:2 * TQ]).astype(BF16)
    o_ref[0, :, LANES:] = jnp.where(lo, o[2 * TQ:3 * TQ], o[3 * TQ:]).astype(BF16)


def _fox_prompt(q, k, v, fcol, frow, mask2):
    b, t, _ = q.shape
    return pl.pallas_call(
        _fox_prompt_kernel,
        grid=(b, t // TQ),
        in_specs=[pl.BlockSpec((1, TQ, 256), lambda bi, i: (bi, i, 0)),
                  pl.BlockSpec((1, t, LANES), lambda bi, i: (bi, 0, 0)),
                  pl.BlockSpec((1, t, LANES), lambda bi, i: (bi, 0, 0)),
                  pl.BlockSpec((1, TQ, N_HEADS), lambda bi, i: (bi, i, 0)),
                  pl.BlockSpec((1, 8, t), lambda bi, i: (bi, 0, 0)),
                  pl.BlockSpec(mask2.shape, lambda bi, i: (0, 0, 0))],
        out_specs=pl.BlockSpec((1, TQ, 256), lambda bi, i: (bi, i, 0)),
        out_shape=jax.ShapeDtypeStruct((b, t, 256), BF16),
        scratch_shapes=[pltpu.VMEM((t, LANES), BF16), pltpu.VMEM((t, LANES), BF16)],
        compiler_params=_cparams(("arbitrary", "arbitrary")),
        name="fox_prompt",
    )(q, k, v, fcol, frow, mask2)


def _moba_select(g, n_valid, n_blocks):
    lane = lax.broadcasted_iota(jnp.int32, g.shape, 1)
    gm = jnp.where(lane < n_valid, g, NEG_INF)
    rank = jnp.zeros(g.shape, F32)
    for mth in range(n_blocks):
        col = gm[:, mth:mth + 1]
        beats = (col > gm) | ((col == gm) & (lane > mth))
        rank = rank + jnp.where(beats, 1.0, 0.0)
    return (rank < float(MOBA_TOPK)) & (lane < n_valid)


def _moba_prompt_kernel(q_ref, qf_ref, k_ref, v_ref, bias_ref, o_ref, kb_ref, vb_ref, km_ref):
    i = pl.program_id(1)
    t = k_ref.shape[1]
    nb = t // TQ

    @pl.when(i == 0)
    def _():
        kb_ref[:, :LANES] = k_ref[0].astype(BF16)
        row_blk = lax.broadcasted_iota(jnp.int32, (t, LANES), 0) // TQ
        lane = lax.broadcasted_iota(jnp.int32, (t, LANES), 1)
        kb_ref[:, LANES:] = jnp.where(row_blk == lane, 1.0, 0.0).astype(BF16)
        vb_ref[...] = v_ref[0].astype(BF16)
        km_ref[...] = jnp.zeros(km_ref.shape, F32)
        for n in range(nb):
            km_ref[n:n + 1, :] = jnp.mean(k_ref[0, n * TQ:(n + 1) * TQ, :], axis=0, keepdims=True)

    q = q_ref[0]
    qf = qf_ref[0]
    qbd = jnp.concatenate([_blockdiag_rows(q[:, :LANES], 2), _blockdiag_rows(q[:, LANES:], 2)], axis=0)
    qfbd = jnp.concatenate([_blockdiag_rows(qf[:, :LANES], 2), _blockdiag_rows(qf[:, LANES:], 2)], axis=0)
    gate = _dot_nt(qfbd, km_ref[...], precision=lax.Precision.HIGHEST)
    lane = lax.broadcasted_iota(jnp.int32, gate.shape, 1)
    keep = _moba_select(gate, i, nb) | (lane >= i)
    code = jnp.where(keep, 0.0, NEG_INF).astype(BF16)
    qaug = jnp.concatenate([qbd, code], axis=1)

    def body(j, carry):
        m, l, acc = carry
        off = pl.multiple_of(j * TQ, TQ)
        s = _dot_nt(qaug, kb_ref[pl.ds(off, TQ), :])
        dd = jnp.minimum(i - j, 2)
        blocks = [s[r * TQ:(r + 1) * TQ] + bias_ref[h, dd] for r, h in enumerate(HEAD_PERM)]
        return _softmax_step(blocks, m, l, acc, vb_ref[pl.ds(off, TQ), :])

    m, l, acc = lax.fori_loop(0, i + 1, body, _init_state(4 * TQ))
    o = acc / l
    lo = _lane_lo((TQ, LANES))
    o_ref[0, :, :LANES] = jnp.where(lo, o[:TQ], o[TQ:2 * TQ]).astype(BF16)
    o_ref[0, :, LANES:] = jnp.where(lo, o[2 * TQ:3 * TQ], o[3 * TQ:]).astype(BF16)


def _moba_prompt(q, qf, k, v, bias):
    b, t, _ = q.shape
    return pl.pallas_call(
        _moba_prompt_kernel,
        grid=(b, t // TQ),
        in_specs=[pl.BlockSpec((1, TQ, 256), lambda bi, i: (bi, i, 0)),
                  pl.BlockSpec((1, TQ, 256), lambda bi, i: (bi, i, 0)),
                  pl.BlockSpec((1, t, LANES), lambda bi, i: (bi, 0, 0)),
                  pl.BlockSpec((1, t, LANES), lambda bi, i: (bi, 0, 0)),
                  pl.BlockSpec(bias.shape, lambda bi, i: (0, 0, 0, 0))],
        out_specs=pl.BlockSpec((1, TQ, 256), lambda bi, i: (bi, i, 0)),
        out_shape=jax.ShapeDtypeStruct((b, t, 256), BF16),
        scratch_shapes=[pltpu.VMEM((t, 2 * LANES), BF16), pltpu.VMEM((t, LANES), BF16),
                        pltpu.VMEM((LANES, LANES), F32)],
        compiler_params=_cparams(("arbitrary", "arbitrary")),
        name="moba_prompt",
    )(q, qf, k, v, bias)


def _diff_finish(m, l, acc, lam, g2, lam_init, t):
    o = acc / l
    o_h0 = o[:t] - lam * o[t:2 * t]
    o_h1 = o[2 * t:3 * t] - lam * o[3 * t:]
    pair = jnp.where(_lane_lo(o_h0.shape), o_h0, o_h1)
    return _subln(pair, g2, lam_init)


def _diff_prompt_kernel(q_ref, k_ref, v_ref, bias_ref, cl_ref, g2_ref, o_ref, kb_ref, vb_ref, *, lam_init):
    i = pl.program_id(1)

    @pl.when(i == 0)
    def _():
        kb_ref[...] = k_ref[0].astype(BF16)
        vb_ref[...] = v_ref[0].astype(BF16)

    lam = _diff_lambda(cl_ref[...], lam_init)
    scale = C_QK_DIM ** -0.5
    for sl in range(2):
        lanes = slice(sl * LANES, (sl + 1) * LANES)
        q4 = _blockdiag_rows(q_ref[0, :, lanes], 4)

        def body(j, carry, lanes=lanes, q4=q4, sl=sl):
            m, l, acc = carry
            off = pl.multiple_of(j * TQ, TQ)
            s = _dot_nt(q4, kb_ref[pl.ds(off, TQ), lanes]) * scale
            dd = jnp.minimum(i - j, 2)
            blocks = [s[r * TQ:(r + 1) * TQ] + bias_ref[2 * sl + r // 2, dd] for r in range(4)]
            return _softmax_step(blocks, m, l, acc, vb_ref[pl.ds(off, TQ), lanes])

        m, l, acc = lax.fori_loop(0, i + 1, body, _init_state(4 * TQ))
        o_ref[0, :, lanes] = _diff_finish(m, l, acc, lam, g2_ref[...], lam_init, TQ).astype(BF16)


def _diff_prompt(q, k, v, bias, cl, g2, lam_init):
    b, t, _ = q.shape
    return pl.pallas_call(
        functools.partial(_diff_prompt_kernel, lam_init=lam_init),
        grid=(b, t // TQ),
        in_specs=[pl.BlockSpec((1, TQ, 256), lambda bi, i: (bi, i, 0)),
                  pl.BlockSpec((1, t, 256), lambda bi, i: (bi, 0, 0)),
                  pl.BlockSpec((1, t, 256), lambda bi, i: (bi, 0, 0)),
                  pl.BlockSpec(bias.shape, lambda bi, i: (0, 0, 0, 0)),
                  pl.BlockSpec(cl.shape, lambda bi, i: (0, 0)),
                  pl.BlockSpec(g2.shape, lambda bi, i: (0, 0))],
        out_specs=pl.BlockSpec((1, TQ, 256), lambda bi, i: (bi, i, 0)),
        out_shape=jax.ShapeDtypeStruct((b, t, 256), BF16),
        scratch_shapes=[pltpu.VMEM((t, 256), BF16), pltpu.VMEM((t, 256), BF16)],
        compiler_params=_cparams(("arbitrary", "arbitrary")),
        name="diff_prompt",
    )(q, k, v, bias, cl, g2)


def _mla_out(o, wuv_ref, t):
    out = _dot(o[:t].astype(BF16), wuv_ref[0])
    for h in range(1, N_HEADS):
        out = out + _dot(o[h * t:(h + 1) * t].astype(BF16), wuv_ref[h])
    return out


def _mla_prompt_kernel(q_ref, ckv_ref, kr_ref, mask_ref, wuv_ref, o_ref, kc_ref):
    i = pl.program_id(1)

    @pl.when(i == 0)
    def _():
        kc_ref[:, :LANES] = ckv_ref[0].astype(BF16)
        kc_ref[:, LANES:] = kr_ref[0].astype(BF16)

    q = jnp.concatenate([q_ref[0, :, h * 256:(h + 1) * 256] for h in range(N_HEADS)], axis=0)
    scale = (D_NOPE + D_ROPE) ** -0.5

    def body(j, carry):
        m, l, acc = carry
        off = pl.multiple_of(j * TQ, TQ)
        s = _dot_nt(q, kc_ref[pl.ds(off, TQ), :]) * scale
        mask = mask_ref[jnp.minimum(i - j, 1)]
        blocks = [s[r * TQ:(r + 1) * TQ] + mask for r in range(N_HEADS)]
        return _softmax_step(blocks, m, l, acc, kc_ref[pl.ds(off, TQ), :LANES])

    m, l, acc = lax.fori_loop(0, i + 1, body, _init_state(4 * TQ))
    o_ref[0] = _mla_out(acc / l, wuv_ref, TQ).astype(BF16)


def _mla_prompt(qd, ckv, kr, mask2, wuv):
    b, t, _ = qd.shape
    return pl.pallas_call(
        _mla_prompt_kernel,
        grid=(b, t // TQ),
        in_specs=[pl.BlockSpec((1, TQ, 1024), lambda bi, i: (bi, i, 0)),
                  pl.BlockSpec((1, t, LANES), lambda bi, i: (bi, 0, 0)),
                  pl.BlockSpec((1, t, LANES), lambda bi, i: (bi, 0, 0)),
                  pl.BlockSpec(mask2.shape, lambda bi, i: (0, 0, 0)),
                  pl.BlockSpec(wuv.shape, lambda bi, i: (0, 0, 0))],
        out_specs=pl.BlockSpec((1, TQ, 256), lambda bi, i: (bi, i, 0)),
        out_shape=jax.ShapeDtypeStruct((b, t, 256), BF16),
        scratch_shapes=[pltpu.VMEM((t, 2 * LANES), BF16)],
        compiler_params=_cparams(("arbitrary", "arbitrary")),
        name="mla_prompt",
    )(qd, ckv, kr, mask2, wuv)


def _page_specs(layer, width, n_chunks, reverse=False):
    specs = []
    for j in range(PAGES_PER_STEP):
        def imap(b, c, pt, j=j):
            cl = jnp.maximum(c - 1, 0)
            if reverse:
                cl = n_chunks - 1 - cl
            return (layer, pt[b, cl * PAGES_PER_STEP + j], 0, 0)
        specs.append(pl.BlockSpec((None, None, LANES, width), imap))
    return specs


def _cat_pages(refs, lanes=None):
    if lanes is None:
        return jnp.concatenate([r[...] for r in refs], axis=0)
    return jnp.concatenate([r[:, lanes] for r in refs], axis=0)


def _state_update(m_ref, l_ref, acc_ref, s_blocks, v, first):
    rows = acc_ref.shape[0]
    if first:
        m, l, acc = _init_state(rows)
    else:
        m, l, acc = m_ref[...], l_ref[...], acc_ref[...]
    m, l, acc = _softmax_step(s_blocks, m, l, acc, v)
    m_ref[...] = m
    l_ref[...] = l
    acc_ref[...] = acc


def _fox_decode_kernel(pt_ref, q_ref, kn_ref, vn_ref, lfn_ref, own_ref, *rest):
    P = PAGES_PER_STEP
    k_refs, v_refs, lf_refs = rest[:P], rest[P:2 * P], rest[2 * P:3 * P]
    o_ref = rest[3 * P]
    m_ref, l_ref, acc_ref, cq_ref, s_ref = rest[3 * P + 1:]
    c = pl.program_id(1)
    nq = own_ref.shape[0]
    q = q_ref[0]

    @pl.when(c == 0)
    def _():
        x = lfn_ref[0]
        lane = lax.broadcasted_iota(jnp.int32, x.shape, 1)
        s = 1
        while s < nq:
            x = x + jnp.where(lane >= s, pltpu.roll(x, s, 1), 0.0)
            s *= 2
        eye = lax.broadcasted_iota(jnp.int32, (nq, LANES), 0) == lax.broadcasted_iota(jnp.int32, (nq, LANES), 1)
        cols = []
        for h in HEAD_PERM:
            cols.append(jnp.sum(jnp.where(eye, x[h:h + 1, :], 0.0), axis=-1, keepdims=True))
        cq = jnp.concatenate(cols, axis=0)
        cq_ref[...] = cq
        s_ref[...] = jnp.zeros(s_ref.shape, F32)
        sc = _dot_nt(q, kn_ref[0].astype(BF16))
        blocks = []
        for r, h in enumerate(HEAD_PERM):
            blocks.append(sc[r * nq:(r + 1) * nq] + (cq[r * nq:(r + 1) * nq] - x[h:h + 1, :]) + own_ref[...])
        _state_update(m_ref, l_ref, acc_ref, blocks, vn_ref[0].astype(BF16), first=True)

    @pl.when(c > 0)
    def _():
        x = jnp.concatenate([r[...] for r in lf_refs], axis=0)
        lane = lax.broadcasted_iota(jnp.int32, x.shape, 1)
        incl = x
        s = 1
        while s < LANES:
            incl = incl + jnp.where(lane < LANES - s, pltpu.roll(incl, LANES - s, 1), 0.0)
            s *= 2
        excl = incl - x
        carry = s_ref[...]
        decay = [None] * P
        for j in range(P - 1, -1, -1):
            decay[j] = excl[j * 8:(j + 1) * 8] + carry
            carry = carry + incl[j * 8:(j + 1) * 8, 0:1]
        s_ref[...] = carry
        sc = _dot_nt(q, _cat_pages(k_refs).astype(BF16))
        cq = cq_ref[...]
        blocks = []
        for r, h in enumerate(HEAD_PERM):
            d = jnp.concatenate([decay[j][h:h + 1, :] for j in range(P)], axis=1)
            blocks.append(sc[r * nq:(r + 1) * nq] + (cq[r * nq:(r + 1) * nq] + d))
        _state_update(m_ref, l_ref, acc_ref, blocks, _cat_pages(v_refs).astype(BF16), first=False)

    @pl.when(c == pl.num_programs(1) - 1)
    def _():
        o_ref[0] = acc_ref[...] / l_ref[...]


def _fox_decode(layer, page_table, q, kn, vn, lfn, own, cache_k, cache_v, cache_lf):
    b, rows, _ = q.shape
    n_chunks = page_table.shape[1] // PAGES_PER_STEP
    bmap = lambda bi, c, pt: (bi, 0, 0)
    in_specs = ([pl.BlockSpec((1, rows, LANES), bmap), pl.BlockSpec((1, LANES, LANES), bmap),
                 pl.BlockSpec((1, LANES, LANES), bmap), pl.BlockSpec((1, 8, LANES), bmap),
                 pl.BlockSpec(own.shape, lambda bi, c, pt: (0, 0))]
                + _page_specs(layer, LANES, n_chunks, True) + _page_specs(layer, LANES, n_chunks, True)
                + [pl.BlockSpec((None, None, 8, LANES), s.index_map)
                   for s in _page_specs(layer, LANES, n_chunks, True)])
    P = PAGES_PER_STEP
    return pl.pallas_call(
        _fox_decode_kernel,
        grid_spec=pltpu.PrefetchScalarGridSpec(
            num_scalar_prefetch=1, grid=(b, n_chunks + 1), in_specs=in_specs,
            out_specs=pl.BlockSpec((1, rows, LANES), bmap),
            scratch_shapes=[pltpu.VMEM((rows, 1), F32), pltpu.VMEM((rows, 1), F32),
                            pltpu.VMEM((rows, LANES), F32), pltpu.VMEM((rows, 1), F32),
                            pltpu.VMEM((8, LANES), F32)]),
        out_shape=jax.ShapeDtypeStruct((b, rows, LANES), F32),
        compiler_params=_cparams(("arbitrary", "arbitrary")),
        name="fox_decode",
    )(page_table, q, kn, vn, lfn, own, *([cache_k] * P), *([cache_v] * P), *([cache_lf] * P))


def _moba_decode_kernel(pt_ref, q_ref, qf_ref, bias_ref, *rest):
    P = PAGES_PER_STEP
    k_refs, v_refs = rest[:P], rest[P:2 * P]
    g_ref, m_ref, l_ref, acc_ref = rest[2 * P:]
    c = pl.program_id(1)
    nq = bias_ref.shape[2]
    q = q_ref[0]
    nbc = P // 2
    last = jnp.where(c == pl.num_programs(1) - 1, 1, 0)
    means = []
    for n in range(nbc):
        k = jnp.concatenate([k_refs[2 * n][...], k_refs[2 * n + 1][...]], axis=0)
        v = jnp.concatenate([v_refs[2 * n][...], v_refs[2 * n + 1][...]], axis=0)
        means.append(jnp.mean(k, axis=0, keepdims=True))
        s = _dot_nt(q, k.astype(BF16))
        dd = last if n == nbc - 1 else 0
        blocks = [s[r * nq:(r + 1) * nq] + bias_ref[h, dd] for r, h in enumerate(HEAD_PERM)]
        m, l, acc = _softmax_step(blocks, *_init_state(4 * nq), v.astype(BF16))
        m_ref[0, 0, n] = m
        l_ref[0, 0, n] = l
        acc_ref[0, 0, n] = acc
    km = jnp.concatenate(means + [jnp.zeros((LANES - nbc, LANES), F32)], axis=0)
    g_ref[0, 0] = _dot_nt(qf_ref[0], km, precision=lax.Precision.HIGHEST)


def _moba_decode(layer, page_table, q, qf, bias, cache_k, cache_v):
    b, rows, _ = q.shape
    P = PAGES_PER_STEP
    n_chunks = page_table.shape[1] // P
    nbc = P // 2
    bmap = lambda bi, c, pt: (bi, 0, 0)

    def specs(width):
        out = []
        for j in range(P):
            out.append(pl.BlockSpec((None, None, LANES, width),
                                    lambda bi, c, pt, j=j: (layer, pt[bi, c * P + j], 0, 0)))
        return out

    return pl.pallas_call(
        _moba_decode_kernel,
        grid_spec=pltpu.PrefetchScalarGridSpec(
            num_scalar_prefetch=1, grid=(b, n_chunks),
            in_specs=[pl.BlockSpec((1, rows, LANES), bmap), pl.BlockSpec((1, rows, LANES), bmap),
                      pl.BlockSpec(bias.shape, lambda bi, c, pt: (0, 0, 0, 0))] + specs(LANES) + specs(LANES),
            out_specs=[pl.BlockSpec((1, 1, rows, LANES), lambda bi, c, pt: (bi, c, 0, 0)),
                       pl.BlockSpec((1, 1, nbc, rows, 1), lambda bi, c, pt: (bi, c, 0, 0, 0)),
                       pl.BlockSpec((1, 1, nbc, rows, 1), lambda bi, c, pt: (bi, c, 0, 0, 0)),
                       pl.BlockSpec((1, 1, nbc, rows, LANES), lambda bi, c, pt: (bi, c, 0, 0, 0))]),
        out_shape=[jax.ShapeDtypeStruct((b, n_chunks, rows, LANES), F32),
                   jax.ShapeDtypeStruct((b, n_chunks, nbc, rows, 1), F32),
                   jax.ShapeDtypeStruct((b, n_chunks, nbc, rows, 1), F32),
                   jax.ShapeDtypeStruct((b, n_chunks, nbc, rows, LANES), F32)],
        compiler_params=_cparams(("arbitrary", "arbitrary")),
        name="moba_decode",
    )(page_table, q, qf, bias, *([cache_k] * P), *([cache_v] * P))


def _moba_combine_kernel(g_ref, mx_ref, lx_ref, acc_ref, q_ref, kn_ref, vn_ref, own_ref, o_ref):
    g = g_ref[0]
    rows, nb = g.shape
    nq = own_ref.shape[1]
    lane = lax.broadcasted_iota(jnp.int32, g.shape, 1)
    rank = jnp.zeros(g.shape, F32)
    for mth in range(nb):
        col = g[:, mth:mth + 1]
        beats = (col > g) | ((col == g) & (lane > mth))
        rank = rank + jnp.where(beats, 1.0, 0.0)
    sel = rank < float(MOBA_TOPK)
    s = _dot_nt(q_ref[0], kn_ref[0].astype(BF16))
    blocks = [s[r * nq:(r + 1) * nq] + own_ref[h] for r, h in enumerate(HEAD_PERM)]
    m_o, l_o, acc_o = _softmax_step(blocks, *_init_state(rows), vn_ref[0].astype(BF16))
    mx = mx_ref[0]
    m_all = jnp.maximum(jnp.max(jnp.where(sel, mx, -jnp.inf), axis=-1, keepdims=True), m_o)
    w = jnp.where(sel, jnp.exp(mx - m_all), 0.0)
    w_o = jnp.exp(m_o - m_all)
    den = jnp.sum(w * lx_ref[0], axis=-1, keepdims=True) + w_o * l_o
    num = w_o * acc_o
    for n in range(nb):
        num = num + w[:, n:n + 1] * acc_ref[0, n]
    o_ref[0] = num / den


def _moba_combine(g, mx, lx, acc, q, kn, vn, own):
    b, rows, nb = g.shape
    bmap = lambda bi: (bi, 0, 0)
    return pl.pallas_call(
        _moba_combine_kernel,
        grid=(b,),
        in_specs=[pl.BlockSpec((1, rows, nb), bmap), pl.BlockSpec((1, rows, nb), bmap),
                  pl.BlockSpec((1, rows, nb), bmap),
                  pl.BlockSpec((1, nb, rows, LANES), lambda bi: (bi, 0, 0, 0)),
                  pl.BlockSpec((1, rows, LANES), bmap), pl.BlockSpec((1, LANES, LANES), bmap),
                  pl.BlockSpec((1, LANES, LANES), bmap),
                  pl.BlockSpec(own.shape, lambda bi: (0, 0, 0))],
        out_specs=pl.BlockSpec((1, rows, LANES), bmap),
        out_shape=jax.ShapeDtypeStruct((b, rows, LANES), F32),
        compiler_params=_cparams(("arbitrary",)),
        name="moba_combine",
    )(g, mx, lx, acc, q, kn, vn, own)


def _diff_decode_kernel(pt_ref, q_ref, kn_ref, vn_ref, own_ref, bias_ref, cl_ref, g2_ref, *rest, lam_init):
    P = PAGES_PER_STEP
    k_refs, v_refs = rest[:P], rest[P:2 * P]
    o_ref = rest[2 * P]
    m_ref, l_ref, acc_ref = rest[2 * P + 1:]
    c = pl.program_id(1)
    nq = own_ref.shape[1]
    scale = C_QK_DIM ** -0.5
    last = jnp.where(c == pl.num_programs(1) - 1, 1, 0)

    for sl in range(2):
        lanes = slice(sl * LANES, (sl + 1) * LANES)
        q4 = q_ref[0, sl]
        heads = [2 * sl + r // 2 for r in range(4)]

        @pl.when(c == 0)
        def _(lanes=lanes, q4=q4, heads=heads, sl=sl):
            s = _dot_nt(q4, kn_ref[0, :, lanes].astype(BF16)) * scale
            blocks = [s[r * nq:(r + 1) * nq] + own_ref[h] for r, h in enumerate(heads)]
            _state_update(m_ref.at[sl], l_ref.at[sl], acc_ref.at[sl], blocks,
                          vn_ref[0, :, lanes].astype(BF16), first=True)

        @pl.when(c > 0)
        def _(lanes=lanes, q4=q4, heads=heads, sl=sl):
            s = _dot_nt(q4, _cat_pages(k_refs, lanes).astype(BF16)) * scale
            blocks = []
            for r, h in enumerate(heads):
                bias = jnp.concatenate([bias_ref[h, 0]] * (P - 1) + [bias_ref[h, last]], axis=1)
                blocks.append(s[r * nq:(r + 1) * nq] + bias)
            _state_update(m_ref.at[sl], l_ref.at[sl], acc_ref.at[sl], blocks,
                          _cat_pages(v_refs, lanes).astype(BF16), first=False)

    @pl.when(c == pl.num_programs(1) - 1)
    def _():
        lam = _diff_lambda(cl_ref[...], lam_init)
        for sl in range(2):
            o_ref[0, :, sl * LANES:(sl + 1) * LANES] = _diff_finish(
                m_ref[sl], l_ref[sl], acc_ref[sl], lam, g2_ref[...], lam_init, nq).astype(BF16)


def _fwd_page_specs(layer, width):
    P = PAGES_PER_STEP
    out = []
    for j in range(P):
        out.append(pl.BlockSpec(
            (None, None, LANES, width),
            lambda bi, c, pt, j=j: (layer, pt[bi, jnp.maximum(c - 1, 0) * P + j], 0, 0)))
    return out


def _diff_decode(layer, page_table, q, kn, vn, own, bias, cl, g2, cache_k, cache_v, lam_init):
    b, _, rows, _ = q.shape
    nq = rows // 4
    P = PAGES_PER_STEP
    n_chunks = page_table.shape[1] // P
    bmap = lambda bi, c, pt: (bi, 0, 0)
    return pl.pallas_call(
        functools.partial(_diff_decode_kernel, lam_init=lam_init),
        grid_spec=pltpu.PrefetchScalarGridSpec(
            num_scalar_prefetch=1, grid=(b, n_chunks + 1),
            in_specs=[pl.BlockSpec((1, 2, rows, LANES), lambda bi, c, pt: (bi, 0, 0, 0)),
                      pl.BlockSpec((1, LANES, 256), bmap), pl.BlockSpec((1, LANES, 256), bmap),
                      pl.BlockSpec(own.shape, lambda bi, c, pt: (0, 0, 0)),
                      pl.BlockSpec(bias.shape, lambda bi, c, pt: (0, 0, 0, 0)),
                      pl.BlockSpec(cl.shape, lambda bi, c, pt: (0, 0)),
                      pl.BlockSpec(g2.shape, lambda bi, c, pt: (0, 0))]
            + _fwd_page_specs(layer, 256) + _fwd_page_specs(layer, 256),
            out_specs=pl.BlockSpec((1, nq, 256), bmap),
            scratch_shapes=[pltpu.VMEM((2, rows, 1), F32), pltpu.VMEM((2, rows, 1), F32),
                            pltpu.VMEM((2, rows, LANES), F32)]),
        out_shape=jax.ShapeDtypeStruct((b, nq, 256), BF16),
        compiler_params=_cparams(("arbitrary", "arbitrary")),
        name="diff_decode",
    )(page_table, q, kn, vn, own, bias, cl, g2, *([cache_k] * P), *([cache_v] * P))


def _mla_decode_kernel(pt_ref, ql_ref, qr_ref, cn_ref, rn_ref, own_ref, wuv_ref, *rest):
    P = PAGES_PER_STEP
    c_refs, r_refs = rest[:P], rest[P:2 * P]
    o_ref = rest[2 * P]
    m_ref, l_ref, acc_ref = rest[2 * P + 1:]
    c = pl.program_id(1)
    nq = own_ref.shape[0]
    scale = (D_NOPE + D_ROPE) ** -0.5
    ql = ql_ref[0]
    qr = qr_ref[0]

    @pl.when(c == 0)
    def _():
        ckv = cn_ref[0].astype(BF16)
        s = (_dot_nt(ql, ckv) + _dot_nt(qr, rn_ref[0].astype(BF16))) * scale
        blocks = [s[r * nq:(r + 1) * nq] + own_ref[...] for r in range(N_HEADS)]
        _state_update(m_ref, l_ref, acc_ref, blocks, ckv, first=True)

    @pl.when(c > 0)
    def _():
        ckv = _cat_pages(c_refs).astype(BF16)
        s = (_dot_nt(ql, ckv) + _dot_nt(qr, _cat_pages(r_refs).astype(BF16))) * scale
        blocks = [s[r * nq:(r + 1) * nq] for r in range(N_HEADS)]
        _state_update(m_ref, l_ref, acc_ref, blocks, ckv, first=False)

    @pl.when(c == pl.num_programs(1) - 1)
    def _():
        o_ref[0] = _mla_out(acc_ref[...] / l_ref[...], wuv_ref, nq).astype(BF16)


def _mla_decode(layer, page_table, ql, qr, cn, rn, own, wuv, cache_ckv, cache_kr):
    b, rows, _ = ql.shape
    nq = rows // N_HEADS
    P = PAGES_PER_STEP
    n_chunks = page_table.shape[1] // P
    bmap = lambda bi, c, pt: (bi, 0, 0)
    return pl.pallas_call(
        _mla_decode_kernel,
        grid_spec=pltpu.PrefetchScalarGridSpec(
            num_scalar_prefetch=1, grid=(b, n_chunks + 1),
            in_specs=[pl.BlockSpec((1, rows, LANES), bmap), pl.BlockSpec((1, rows, D_ROPE), bmap),
                      pl.BlockSpec((1, LANES, LANES), bmap), pl.BlockSpec((1, LANES, D_ROPE), bmap),
                      pl.BlockSpec(own.shape, lambda bi, c, pt: (0, 0)),
                      pl.BlockSpec(wuv.shape, lambda bi, c, pt: (0, 0, 0))]
            + _fwd_page_specs(layer, LANES) + _fwd_page_specs(layer, D_ROPE),
            out_specs=pl.BlockSpec((1, nq, 256), bmap),
            scratch_shapes=[pltpu.VMEM((rows, 1), F32), pltpu.VMEM((rows, 1), F32),
                            pltpu.VMEM((rows, LANES), F32)]),
        out_shape=jax.ShapeDtypeStruct((b, nq, 256), BF16),
        compiler_params=_cparams(("arbitrary", "arbitrary")),
        name="mla_decode",
    )(page_table, ql, qr, cn, rn, own, wuv, *([cache_ckv] * P), *([cache_kr] * P))


def _bucket_table():
    d = np.arange(MAX_DISTANCE + 1)
    max_exact = N_BUCKETS // 2
    df = np.maximum(d, 1).astype(np.float32)
    large = max_exact + (np.log(df / max_exact) / math.log(MAX_DISTANCE / max_exact)
                         * (N_BUCKETS - max_exact)).astype(np.int32)
    large = np.minimum(large, N_BUCKETS - 1)
    return np.where(d < max_exact, d, large)


def _bias_tables(rel_bias, nq, q0):
    bd = rel_bias[_bucket_table()].T
    r = np.arange(TQ)[:, None]
    c = np.arange(TQ)[None, :]
    idx0 = np.clip(r - c, 0, MAX_DISTANCE)
    idx1 = np.clip(TQ + r - c, 0, MAX_DISTANCE)
    far = jnp.broadcast_to(bd[:, MAX_DISTANCE][:, None, None], (bd.shape[0], TQ, TQ))
    prompt = jnp.stack([jnp.where(r >= c, bd[:, idx0], NEG_INF), bd[:, idx1], far], axis=1)
    t = np.arange(nq)[:, None]
    u = np.arange(LANES)[None, :]
    own_ok = (u <= t) & (u < nq)
    own = jnp.where(own_ok, bd[:, np.clip(t - u, 0, MAX_DISTANCE)], NEG_INF)
    own_mask = jnp.where(own_ok, 0.0, NEG_INF).astype(F32)
    idx_last = np.clip(LANES + t - u, 0, MAX_DISTANCE)
    page_far = jnp.broadcast_to(bd[:, MAX_DISTANCE][:, None, None], (bd.shape[0], nq, LANES))
    page = jnp.stack([page_far, bd[:, idx_last]], axis=1)
    blk = jnp.stack([jnp.concatenate([page_far, page_far], axis=-1),
                     jnp.concatenate([page_far, bd[:, idx_last]], axis=-1)], axis=1)
    mask2 = jnp.stack([jnp.where(r >= c, 0.0, NEG_INF).astype(F32), jnp.zeros((TQ, TQ), F32)])
    return dict(prompt=prompt, own=own, own_mask=own_mask, page=page, blk=blk, mask2=mask2)


def _rope_tables(pos):
    half = D_ROPE // 2
    inv = jnp.power(ROPE_BASE, -jnp.arange(half, dtype=F32) / half)
    ang = pos.astype(F32)[:, None] * inv
    c, s = jnp.cos(ang), jnp.sin(ang)
    pad = jnp.zeros((pos.shape[0], LANES - D_ROPE), F32)
    return jnp.concatenate([c, c, pad], axis=1), jnp.concatenate([-s, s, pad], axis=1)


def _perm_heads(w, axis):
    parts = jnp.split(w, N_HEADS, axis=axis)
    return jnp.concatenate([parts[h] for h in HEAD_PERM], axis=axis)


def _layer_weights(l, norm_g, w_in, b_forget, b_gate, d_q_norm_g, d_w_q_up, d_kv_norm_g, d_w_kv_up,
                   c_subln_g, w_branch, w_out):
    w = w_in[l]
    d = w.shape[0]
    splits = (256, 128, 128, 4, 256, 256, 128, 128, 256, 256, 256, 256, 256, 256, 128, 32, 256, 4 * d)
    offs = np.cumsum((0,) + splits)
    (a_q, a_k, a_v, a_f, a_z, b_q, b_k, b_v, b_z, c_q, c_k, c_v, c_z, d_qa, d_kva, d_kr, d_z, gates) = [
        w[:, offs[i]:offs[i + 1]] for i in range(len(splits))]
    scale = HEAD_DIM ** -0.5
    zpad = lambda x, n: jnp.concatenate([x, jnp.zeros((d, n - x.shape[1]), x.dtype)], axis=1)
    swap = jnp.concatenate([d_kr[:, D_ROPE // 2:], d_kr[:, :D_ROPE // 2]], axis=1)
    segs = dict(qa=_perm_heads(a_q, 1) * scale, ka=a_k, va=a_v, qb=_perm_heads(b_q, 1) * scale, kb=b_k, vb=b_v,
                qc=c_q, kc=c_k, vc=c_v, dqa=d_qa, dkva=d_kva, kr=zpad(d_kr, LANES), krs=zpad(swap, LANES),
                af=zpad(a_f, LANES),
                z=jnp.concatenate([_perm_heads(a_z, 1), _perm_heads(b_z, 1), c_z, d_z], axis=1))
    w_pack = jnp.concatenate([segs[n] for n, _ in _SEGS], axis=1).astype(BF16)
    wq = d_w_q_up[l]
    r = wq.shape[0]
    nope = wq[:, :, :D_NOPE].reshape(r, N_HEADS * D_NOPE)
    rope = wq[:, :, D_NOPE:]
    rope_sw = jnp.concatenate([rope[..., D_ROPE // 2:], rope[..., :D_ROPE // 2]], axis=-1)

    def spread(x):
        z1 = jnp.zeros((r, N_HEADS, D_LAT), x.dtype)
        z2 = jnp.zeros((r, N_HEADS, 256 - D_LAT - D_ROPE), x.dtype)
        return jnp.concatenate([z1, x, z2], axis=-1).reshape(r, N_HEADS * 256)

    wq_pack = jnp.concatenate([nope, spread(rope), spread(rope_sw)], axis=1).astype(BF16)
    wkv = d_w_kv_up[l]
    w_uk = wkv[:, :, :D_NOPE]
    w_uv = wkv[:, :, D_NOPE:]
    wuk = jnp.zeros((N_HEADS, D_NOPE, N_HEADS, 256), F32)
    wuv = jnp.zeros((N_HEADS, D_LAT, N_HEADS, D_V), F32)
    for h in range(N_HEADS):
        wuk = wuk.at[h, :, h, :D_LAT].set(w_uk[:, h, :].T)
        wuv = wuv.at[h, :, h, :].set(w_uv[:, h, :])
    wbr = w_branch[l]
    wbr = jnp.stack([_perm_heads(wbr[0], 0), _perm_heads(wbr[1], 0), wbr[2], wbr[3]])
    return dict(
        norm_g=norm_g[l][None, :], w_pack=w_pack,
        b_forget=jnp.concatenate([b_forget[l].astype(F32), jnp.zeros((LANES - N_HEADS,), F32)])[None, :],
        gq=d_q_norm_g[l][None, :], gkv=d_kv_norm_g[l][None, :], wq=wq_pack,
        wuk=wuk.reshape(N_HEADS * D_NOPE, N_HEADS * 256).astype(BF16),
        wuv=wuv.reshape(N_HEADS, D_LAT, N_HEADS * D_V).astype(BF16),
        wg=gates.astype(BF16), b_gate=b_gate[l][None, :], wbr=wbr.astype(BF16), wout=w_out[l].astype(BF16),
        g2=jnp.concatenate([c_subln_g[l], c_subln_g[l]])[None, :])


def _bd_rows_host(q, n_groups):
    lane = jnp.arange(LANES)
    gw = LANES // n_groups
    parts = [jnp.where((lane >= g * gw) & (lane < (g + 1) * gw), q, jnp.zeros_like(q)) for g in range(n_groups)]
    return jnp.concatenate(parts, axis=1)


def _unbd(o, nq):
    lo = jnp.arange(LANES) < LANES // 2
    left = jnp.where(lo, o[:, :nq], o[:, nq:2 * nq])
    right = jnp.where(lo, o[:, 2 * nq:3 * nq], o[:, 3 * nq:])
    return jnp.concatenate([left, right], axis=-1).reshape(-1, 256)


def _pad_page(x):
    return jnp.pad(x, ((0, 0), (0, LANES - x.shape[1]), (0, 0)))


def kernel(x_prompt, x_sample, cache_a_k, cache_a_v, cache_a_logf, cache_b_k, cache_b_v, cache_c_k, cache_c_v, cache_d_ckv, cache_d_kr, page_table, norm_g, w_in, b_forget, b_gate, d_q_norm_g, d_w_q_up, d_kv_norm_g, d_w_kv_up, c_lambda, c_subln_g, w_branch, w_out, rel_bias, final_norm_g):
    bp, t, d = x_prompt.shape
    bs, nq, _ = x_sample.shape
    depth, n_phys, page = cache_a_k.shape[:3]
    n_pages = page_table.shape[1]
    q0 = n_pages * page
    assert page == LANES and t % TQ == 0 and n_pages % PAGES_PER_STEP == 0 and nq % 8 == 0

    flat = lambda c: c.reshape(c.shape[:3] + (-1,))
    ca_k, ca_v, cb_k, cb_v, cc_k, cc_v = map(flat, (cache_a_k, cache_a_v, cache_b_k, cache_b_v, cache_c_k, cache_c_v))
    ca_lf = jnp.pad(jnp.swapaxes(cache_a_logf, 2, 3), ((0, 0), (0, 0), (0, 8 - N_HEADS), (0, 0)))

    tabs = _bias_tables(rel_bias.astype(F32), nq, q0)
    cos_p, sin_p = _rope_tables(jnp.arange(t))
    cos_s, sin_s = _rope_tables(q0 + jnp.arange(nq))
    tm_p = 256
    tm_s = bs * nq
    cos_s = jnp.tile(cos_s, (bs, 1))
    sin_s = jnp.tile(sin_s, (bs, 1))
    final_g = final_norm_g[None, :]

    hp = x_prompt.reshape(bp * t, d)
    hs = x_sample.reshape(bs * nq, d)
    rows_p, rows_s = [], []
    for l in range(depth):
        lam_init = 0.8 - 0.6 * math.exp(-0.3 * l)
        lw = _layer_weights(l, norm_g, w_in, b_forget, b_gate, d_q_norm_g, d_w_q_up, d_kv_norm_g, d_w_kv_up,
                            c_subln_g, w_branch, w_out)
        cl = c_lambda[l].astype(F32)
        last = l == depth - 1

        pr = _inproj(hp, lw, cos_p, sin_p, tm_p)
        r3 = lambda a: a.reshape(bp, t, a.shape[-1])
        lf = r3(pr["lf"])[:, :, :N_HEADS]
        lf_t = jnp.pad(jnp.swapaxes(lf, 1, 2), ((0, 0), (0, 8 - N_HEADS), (0, 0)))
        frow = _cumsum_rows(lf_t)
        fcol = jnp.swapaxes(frow[:, :N_HEADS], 1, 2)
        o_a = _fox_prompt(r3(pr["qa"]), r3(pr["ka"]), r3(pr["va"]), fcol, frow, tabs["mask2"])
        o_b = _moba_prompt(r3(pr["qb"]), r3(pr["qbf"]), r3(pr["kb"]), r3(pr["vb"]), tabs["prompt"][:N_HEADS])
        o_c = _diff_prompt(r3(pr["qc"]), r3(pr["kc"]), r3(pr["vc"]), tabs["prompt"][N_HEADS:], cl, lw["g2"], lam_init)
        o_d = _mla_prompt(r3(pr["qd"]), r3(pr["ckv"]), r3(pr["kr"]), tabs["mask2"], lw["wuv"])
        f2 = lambda a: a.reshape(bp * t, a.shape[-1])
        hp = _merge(hp, f2(o_a), f2(o_b), f2(o_c), f2(o_d), pr["z"], lw, final_g, last, 512)
        rows_p.append((pr["ka"].reshape(bp, t, KV_HEADS, HEAD_DIM), pr["va"].reshape(bp, t, KV_HEADS, HEAD_DIM), lf,
                       pr["kb"].reshape(bp, t, KV_HEADS, HEAD_DIM), pr["vb"].reshape(bp, t, KV_HEADS, HEAD_DIM),
                       pr["kc"].reshape(bp, t, N_HEADS, 2 * C_QK_DIM), pr["vc"].reshape(bp, t, N_HEADS, HEAD_DIM),
                       r3(pr["ckv"]), r3(pr["kr"])[:, :, :D_ROPE]))

        sr = _inproj(hs, lw, cos_s, sin_s, tm_s)
        s3 = lambda a: a.reshape(bs, nq, a.shape[-1])
        lf_s = s3(sr["lf"])[:, :, :N_HEADS]
        qa = s3(sr["qa"])
        q_bd = jnp.concatenate([_bd_rows_host(qa[..., :LANES], 2), _bd_rows_host(qa[..., LANES:], 2)], axis=1)
        lfn = jnp.pad(jnp.swapaxes(lf_s, 1, 2), ((0, 0), (0, 8 - N_HEADS), (0, LANES - nq)))
        o = _fox_decode(l, page_table, q_bd, _pad_page(s3(sr["ka"])), _pad_page(s3(sr["va"])), lfn,
                        tabs["own_mask"], ca_k, ca_v, ca_lf)
        o_a = _unbd(o, nq).astype(BF16)
        qb = s3(sr["qb"])
        qbf = s3(sr["qbf"])
        q_bd = jnp.concatenate([_bd_rows_host(qb[..., :LANES], 2), _bd_rows_host(qb[..., LANES:], 2)], axis=1)
        qf_bd = jnp.concatenate([_bd_rows_host(qbf[..., :LANES], 2), _bd_rows_host(qbf[..., LANES:], 2)], axis=1)
        g, mx, lx, acc = _moba_decode(l, page_table, q_bd, qf_bd, tabs["blk"][:N_HEADS], cb_k, cb_v)
        nbc = PAGES_PER_STEP // 2
        rows = 4 * nq
        g = jnp.swapaxes(g[..., :nbc], 1, 2).reshape(bs, rows, -1)
        mx = jnp.swapaxes(mx.reshape(bs, -1, rows), 1, 2)
        lx = jnp.swapaxes(lx.reshape(bs, -1, rows), 1, 2)
        acc = acc.reshape(bs, -1, rows, LANES)
        o = _moba_combine(g, mx, lx, acc, q_bd, _pad_page(s3(sr["kb"])), _pad_page(s3(sr["vb"])),
                          tabs["own"][:N_HEADS])
        o_b = _unbd(o, nq).astype(BF16)
        qc = s3(sr["qc"])
        q4 = jnp.stack([_bd_rows_host(qc[..., :LANES], 4), _bd_rows_host(qc[..., LANES:], 4)], axis=1)
        o_c = _diff_decode(l, page_table, q4, _pad_page(s3(sr["kc"])), _pad_page(s3(sr["vc"])),
                           tabs["own"][N_HEADS:], tabs["page"][N_HEADS:], cl, lw["g2"], cc_k, cc_v, lam_init)
        o_c = o_c.reshape(bs * nq, 256)
        qd = jnp.swapaxes(s3(sr["qd"]).reshape(bs, nq, N_HEADS, 256), 1, 2).reshape(bs, rows, 256)
        o_d = _mla_decode(l, page_table, qd[..., :D_LAT], qd[..., D_LAT:D_LAT + D_ROPE],
                          _pad_page(s3(sr["ckv"])), _pad_page(s3(sr["kr"])[..., :D_ROPE]),
                          tabs["own_mask"], lw["wuv"], cache_d_ckv, cache_d_kr)
        o_d = o_d.reshape(bs * nq, 256)
        hs = _merge(hs, o_a, o_b, o_c, o_d, sr["z"], lw, final_g, last, tm_s)
        rows_s.append((sr["ka"].reshape(bs, nq, KV_HEADS, HEAD_DIM), sr["va"].reshape(bs, nq, KV_HEADS, HEAD_DIM), lf_s,
                       sr["kb"].reshape(bs, nq, KV_HEADS, HEAD_DIM), sr["vb"].reshape(bs, nq, KV_HEADS, HEAD_DIM),
                       sr["kc"].reshape(bs, nq, N_HEADS, 2 * C_QK_DIM), sr["vc"].reshape(bs, nq, N_HEADS, HEAD_DIM),
                       s3(sr["ckv"]), s3(sr["kr"])[:, :, :D_ROPE]))

    stack = lambda rows, i: jnp.stack([r[i] for r in rows], axis=0)
    return ((hp.reshape(bp, t, d), hs.reshape(bs, nq, d))
            + tuple(stack(rows_p, i) for i in range(9)) + tuple(stack(rows_s, i) for i in range(9)))
```

```python
import functools
import math

import jax
import jax.numpy as jnp
import numpy as np
from jax import lax
from jax.experimental import pallas as pl
from jax.experimental.pallas import tpu as pltpu

F32 = jnp.float32
BF16 = jnp.bfloat16

HEAD_DIM = 64
N_HEADS = 4
KV_HEADS = 2
MOBA_TOPK = 3
C_QK_DIM = 32
D_NOPE = 64
D_ROPE = 32
D_V = 64
D_LAT = 128
ROPE_BASE = 10000.0
N_BUCKETS = 32
MAX_DISTANCE = 128
EPS = 1e-6
NEG_INF = -1e30

LANES = 128
TQ = 256
PAGES_PER_STEP = 16
VMEM_LIMIT = 56 * 1024 * 1024

HEAD_PERM = (0, 2, 1, 3)

_SEGS = (("qa", 256), ("ka", 128), ("va", 128), ("qb", 256), ("kb", 128), ("vb", 128),
         ("qc", 256), ("kc", 256), ("vc", 256), ("dqa", 256), ("dkva", 128),
         ("kr", 128), ("krs", 128), ("af", 128), ("z", 1024))
_SEGS_R = (("ka", 128), ("kb", 128), ("kc", 256), ("dkva", 128), ("kr", 128), ("krs", 128), ("z", 1024))
_SEGS_T = (("qa", 256), ("qb", 256), ("qc", 256), ("ka", 128), ("va", 128), ("kb", 128), ("vb", 128),
           ("kc", 256), ("vc", 256), ("dqa", 256), ("dkva", 128), ("kr", 128), ("krs", 128), ("af", 8))


def _offsets(segs):
    off, o = {}, 0
    for n, w in segs:
        off[n] = (o, o + w)
        o += w
    return off


_OFF = _offsets(_SEGS)
_OFF_R = _offsets(_SEGS_R)
_OFF_T = _offsets(_SEGS_T)


def _cparams(sem):
    return pltpu.CompilerParams(dimension_semantics=sem, vmem_limit_bytes=VMEM_LIMIT)


def _dot(a, b, precision=None):
    return jnp.dot(a, b, preferred_element_type=F32, precision=precision)


def _dot_nt(a, b, precision=None):
    return lax.dot_general(a, b, (((1,), (1,)), ((), ())), preferred_element_type=F32,
                           precision=precision)


def _rms(x, g):
    return x * lax.rsqrt(jnp.mean(x * x, axis=-1, keepdims=True) + EPS) * g


def _rms_t(x, g_col):
    return x * lax.rsqrt(jnp.mean(x * x, axis=0, keepdims=True) + EPS) * g_col


def _log_sigmoid(x):
    return jnp.minimum(x, 0.0) - jnp.log(1.0 + jnp.exp(-jnp.abs(x)))


def _inproj_kernel(x_ref, g_ref, w_ref, bf_ref, gq_ref, gkv_ref, wq_ref, wuk_ref, cos_ref, sin_ref,
                   qa_ref, ka_ref, va_ref, lf_ref, qb_ref, qbf_ref, kb_ref, vb_ref,
                   qc_ref, kc_ref, vc_ref, qd_ref, ckv_ref, kr_ref, z_ref):
    hb = _rms(x_ref[...], g_ref[...]).astype(BF16)
    proj = _dot(hb, w_ref[...])

    def seg(name):
        lo, hi = _OFF[name]
        return proj[:, lo:hi]

    qa_ref[...] = seg("qa").astype(BF16)
    ka_ref[...] = seg("ka")
    va_ref[...] = seg("va")
    lf_ref[...] = _log_sigmoid(seg("af") + bf_ref[...])
    qb = seg("qb")
    qb_ref[...] = qb.astype(BF16)
    qbf_ref[...] = qb
    kb_ref[...] = seg("kb")
    vb_ref[...] = seg("vb")
    qc_ref[...] = seg("qc").astype(BF16)
    kc_ref[...] = seg("kc")
    vc_ref[...] = seg("vc")
    z_ref[...] = seg("z").astype(BF16)
    ckv_ref[...] = _rms(seg("dkva"), gkv_ref[...])
    cos = cos_ref[...]
    sin = sin_ref[...]
    kr_ref[...] = seg("kr") * cos + seg("krs") * sin
    qn = _rms(seg("dqa"), gq_ref[...]).astype(BF16)
    qq = _dot(qn, wq_ref[...])
    q_lat = _dot(qq[:, :256].astype(BF16), wuk_ref[...])
    cos8 = jnp.concatenate([cos] * 8, axis=1)
    sin8 = jnp.concatenate([sin] * 8, axis=1)
    q_rope = qq[:, 256:1280] * cos8 + qq[:, 1280:2304] * sin8
    qd_ref[...] = (q_lat + q_rope).astype(BF16)


def _inproj(x, lw, cos_t, sin_t, tm):
    m = x.shape[0]
    row = lambda w: pl.BlockSpec((tm, w), lambda i: (i, 0))
    full = lambda a: pl.BlockSpec(a.shape, lambda i: (0,) * a.ndim)
    outs = (("qa", 256, BF16), ("ka", 128, F32), ("va", 128, F32), ("lf", 128, F32),
            ("qb", 256, BF16), ("qbf", 256, F32), ("kb", 128, F32), ("vb", 128, F32),
            ("qc", 256, BF16), ("kc", 256, F32), ("vc", 256, F32), ("qd", 1024, BF16),
            ("ckv", 128, F32), ("kr", 128, F32), ("z", 1024, BF16))
    ins = (x, lw["norm_g"], lw["w_pack"], lw["b_forget"], lw["gq"], lw["gkv"], lw["wq"], lw["wuk"], cos_t, sin_t)
    res = pl.pallas_call(
        _inproj_kernel,
        grid=(m // tm,),
        in_specs=[row(x.shape[1])] + [full(a) for a in ins[1:8]] + [row(LANES), row(LANES)],
        out_specs=[row(w) for _, w, _ in outs],
        out_shape=[jax.ShapeDtypeStruct((m, w), dt) for _, w, dt in outs],
        compiler_params=_cparams(("arbitrary",)),
        name="inproj",
    )(*ins)
    return {n: r for (n, _, _), r in zip(outs, res)}


_P_ROW_OUTS = (("ka", 128, BF16), ("kb", 128, F32), ("kc", 256, BF16), ("ckv", 128, F32), ("kr", 128, F32),
               ("z", 1024, BF16))
_P_T_OUTS = (("qaT", 256, BF16), ("qbT", 256, BF16), ("qbfT", 256, F32), ("qcT", 256, BF16), ("qdT", 1024, BF16),
             ("kaT", 128, F32), ("vaT", 128, F32), ("kbT", 128, F32), ("vbT", 128, F32), ("kcT", 256, F32),
             ("vcT", 256, F32), ("ckvT", 128, BF16), ("krT", 128, F32), ("lfT", 8, F32))


def _inproj_prompt_kernel(x_ref, g_ref, wr_ref, wt_ref, bfc_ref, gqc_ref, gkv_ref, gkvc_ref, wqt_ref, wukt_ref,
                          cos_ref, sin_ref, cost_ref, sint_ref, *outs):
    o = {n: r for (n, _, _), r in zip(_P_ROW_OUTS + _P_T_OUTS, outs)}
    hb = _rms(x_ref[...], g_ref[...]).astype(BF16)
    pr = _dot(hb, wr_ref[...])

    def seg(name):
        lo, hi = _OFF_R[name]
        return pr[:, lo:hi]

    o["ka"][...] = seg("ka").astype(BF16)
    o["kb"][...] = seg("kb")
    o["kc"][...] = seg("kc").astype(BF16)
    o["ckv"][...] = _rms(seg("dkva"), gkv_ref[...])
    o["kr"][...] = seg("kr") * cos_ref[...] + seg("krs") * sin_ref[...]
    o["z"][...] = seg("z").astype(BF16)

    pt = _dot_nt(wt_ref[...], hb)

    def segt(name):
        lo, hi = _OFF_T[name]
        return pt[lo:hi]

    o["qaT"][0] = segt("qa").astype(BF16)
    qb = segt("qb")
    o["qbT"][0] = qb.astype(BF16)
    o["qbfT"][0] = qb
    o["qcT"][0] = segt("qc").astype(BF16)
    for n in ("ka", "va", "kb", "vb", "kc", "vc"):
        o[n + "T"][0] = segt(n)
    o["lfT"][0] = _log_sigmoid(segt("af") + bfc_ref[...])
    cost = cost_ref[...]
    sint = sint_ref[...]
    o["krT"][0] = segt("kr") * cost + segt("krs") * sint
    o["ckvT"][0] = _rms_t(segt("dkva"), gkvc_ref[...]).astype(BF16)
    qn = _rms_t(segt("dqa"), gqc_ref[...]).astype(BF16)
    qq = _dot(wqt_ref[...], qn)
    q_lat = _dot(wukt_ref[...], qq[:256].astype(BF16))
    cos8 = jnp.concatenate([cost] * 8, axis=0)
    sin8 = jnp.concatenate([sint] * 8, axis=0)
    o["qdT"][0] = (q_lat + qq[256:1280] * cos8 + qq[1280:2304] * sin8).astype(BF16)


def _inproj_prompt(x, lw, tabs, b, t, tm):
    m = x.shape[0]
    n_t = t // tm
    row = lambda w: pl.BlockSpec((tm, w), lambda i: (i, 0))
    full = lambda a: pl.BlockSpec(a.shape, lambda i: (0,) * a.ndim)
    tr = lambda w: pl.BlockSpec((1, w, tm), lambda i: (i // n_t, 0, i % n_t))
    ins = (x, lw["norm_g"], lw["w_rows"], lw["w_t"], lw["bf_col"], lw["gq_col"], lw["gkv"], lw["gkv_col"],
           lw["wq_t"], lw["wuk_t"], tabs["cos_p"], tabs["sin_p"], tabs["cos_pt"], tabs["sin_pt"])
    res = pl.pallas_call(
        _inproj_prompt_kernel,
        grid=(m // tm,),
        in_specs=[row(x.shape[1])] + [full(a) for a in ins[1:10]]
        + [pl.BlockSpec((tm, LANES), lambda i: (i % n_t, 0))] * 2
        + [pl.BlockSpec((LANES, tm), lambda i: (0, i % n_t))] * 2,
        out_specs=[row(w) for _, w, _ in _P_ROW_OUTS] + [tr(w) for _, w, _ in _P_T_OUTS],
        out_shape=[jax.ShapeDtypeStruct((m, w), dt) for _, w, dt in _P_ROW_OUTS]
        + [jax.ShapeDtypeStruct((b, w, t), dt) for _, w, dt in _P_T_OUTS],
        compiler_params=_cparams(("arbitrary",)),
        name="inproj_prompt",
    )(*ins)
    return {n: r for (n, _, _), r in zip(_P_ROW_OUTS + _P_T_OUTS, res)}


def _merge_kernel(x_ref, oa_ref, ob_ref, oc_ref, od_ref, z_ref, g_ref, wg_ref, bg_ref, wbr_ref,
                  wout_ref, fg_ref, y_ref, *, final):
    x = x_ref[...]
    hb = _rms(x, g_ref[...]).astype(BF16)
    d = x.shape[1]
    acc = jnp.zeros(x.shape, F32)
    for n, o_ref in enumerate((oa_ref, ob_ref, oc_ref, od_ref)):
        w = o_ref.shape[1]
        z = z_ref[:, n * w:(n + 1) * w].astype(F32)
        a = (o_ref[...].astype(F32) * (z * jax.nn.sigmoid(z))).astype(BF16)
        u = _dot(a, wbr_ref[n])
        gate = _dot(hb, wg_ref[:, n * d:(n + 1) * d]) + bg_ref[:, n * d:(n + 1) * d]
        acc = acc + jax.nn.sigmoid(gate) * u
    y = x + _dot(acc.astype(BF16), wout_ref[...])
    if final:
        y = _rms(y, fg_ref[...])
    y_ref[...] = y


def _merge(x, o_a, o_b, o_c, o_d, z, lw, final_g, final, tm):
    m, d = x.shape
    row = lambda w: pl.BlockSpec((tm, w), lambda i: (i, 0))
    full = lambda a: pl.BlockSpec(a.shape, lambda i: (0,) * a.ndim)
    return pl.pallas_call(
        functools.partial(_merge_kernel, final=final),
        grid=(m // tm,),
        in_specs=[row(d), row(256), row(256), row(256), row(256), row(1024), full(lw["norm_g"]),
                  full(lw["wg"]), full(lw["b_gate"]), full(lw["wbr"]), full(lw["wout"]),
                  full(final_g)],
        out_specs=row(d),
        out_shape=jax.ShapeDtypeStruct((m, d), F32),
        compiler_params=_cparams(("arbitrary",)),
        name="merge",
    )(x, o_a, o_b, o_c, o_d, z, lw["norm_g"], lw["wg"], lw["b_gate"], lw["wbr"], lw["wout"],
      final_g)


def _lane_lo(shape):
    return lax.broadcasted_iota(jnp.int32, shape, len(shape) - 1) < (LANES // 2)


def _softmax_step(s_blocks, m, l, acc, v, v_is_t=False):
    t = s_blocks[0].shape[0]
    p_blocks, m_new_blocks, alpha_blocks, l_blocks = [], [], [], []
    for r, s in enumerate(s_blocks):
        m_old = m[r * t:(r + 1) * t]
        m_new = jnp.maximum(m_old, jnp.max(s, axis=-1, keepdims=True))
        p = jnp.exp(s - m_new)
        alpha = jnp.exp(m_old - m_new)
        l_blocks.append(alpha * l[r * t:(r + 1) * t] + jnp.sum(p, axis=-1, keepdims=True))
        p_blocks.append(p.astype(BF16))
        m_new_blocks.append(m_new)
        alpha_blocks.append(alpha)
    p_all = jnp.concatenate(p_blocks, axis=0)
    alpha_all = jnp.concatenate(alpha_blocks, axis=0)
    pv = _dot_nt(p_all, v) if v_is_t else _dot(p_all, v)
    acc = alpha_all * acc + pv
    return jnp.concatenate(m_new_blocks, axis=0), jnp.concatenate(l_blocks, axis=0), acc


def _init_state(rows):
    return (jnp.full((rows, 1), -jnp.inf, F32), jnp.zeros((rows, 1), F32), jnp.zeros((rows, LANES), F32))


def _subln(pair, g2, lam_init):
    lo = _lane_lo(pair.shape)
    sq = pair * pair
    ss_lo = jnp.sum(jnp.where(lo, sq, 0.0), axis=-1, keepdims=True)
    ss_hi = jnp.sum(jnp.where(lo, 0.0, sq), axis=-1, keepdims=True)
    ms = jnp.where(lo, ss_lo, ss_hi) * (1.0 / D_V)
    return (pair * lax.rsqrt(ms + EPS) * g2) * (1.0 - lam_init)


def _diff_lambda(cl, lam_init):
    a = jnp.sum(cl[0:1] * cl[1:2], axis=-1, keepdims=True)
    b = jnp.sum(cl[2:3] * cl[3:4], axis=-1, keepdims=True)
    return jnp.exp(a) - jnp.exp(b) + lam_init


def _sub_mask(x, lo, hi):
    sub = lax.broadcasted_iota(jnp.int32, x.shape, 0)
    return jnp.where((sub >= lo) & (sub < hi), x, jnp.zeros_like(x))


def _blockdiag_cols(x, n_groups):
    gw = x.shape[0] // n_groups
    return jnp.concatenate([_sub_mask(x, g * gw, (g + 1) * gw) for g in range(n_groups)], axis=1)


def _flash_t(i, qt, k_tile, v_tile, bias_chunk, st_ref, pt_ref, acc_ref, scale=None):
    r_all = qt.shape[1]
    acc_ref[...] = jnp.zeros(acc_ref.shape, F32)

    def body(j, carry):
        m, l = carry
        off = pl.multiple_of(j * TQ, TQ)
        st_ref[...] = _dot(k_tile(off), qt)
        ms, ls, alphas = [], [], []
        for c in range(r_all // LANES):
            cs = slice(c * LANES, (c + 1) * LANES)
            s = st_ref[:, cs]
            if scale is not None:
                s = s * scale
            s = bias_chunk(j, c, s)
            m_prev = m[:, cs]
            m_new = jnp.maximum(m_prev, jnp.max(s, axis=0, keepdims=True))
            p = jnp.exp(s - m_new)
            alpha = jnp.exp(m_prev - m_new)
            ls.append(alpha * l[:, cs] + jnp.sum(p, axis=0, keepdims=True))
            pt_ref[:, cs] = p.astype(BF16)
            ms.append(m_new)
            alphas.append(alpha)
        acc_ref[...] = acc_ref[...] * jnp.concatenate(alphas, axis=1) + _dot(v_tile(off), pt_ref[...])
        return jnp.concatenate(ms, axis=1), jnp.concatenate(ls, axis=1)

    init = (jnp.full((1, r_all), -jnp.inf, F32), jnp.zeros((1, r_all), F32))
    _, l = lax.fori_loop(0, i + 1, body, init)
    return l


def _eye_bf16(n):
    return jnp.where(lax.broadcasted_iota(jnp.int32, (n, n), 0) == lax.broadcasted_iota(jnp.int32, (n, n), 1),
                     1.0, 0.0).astype(BF16)


def _to_rows(x_t):
    return _dot_nt(_eye_bf16(x_t.shape[1]), x_t.astype(BF16)).astype(BF16)


def _pair_halves(o, s):
    sub = lax.broadcasted_iota(jnp.int32, (LANES, TQ), 0)
    return jnp.where(sub < LANES // 2, o[:, 2 * s * TQ:(2 * s + 1) * TQ], o[:, (2 * s + 1) * TQ:(2 * s + 2) * TQ])


_T_SCRATCH = lambda: [pltpu.VMEM((TQ, 4 * TQ), F32), pltpu.VMEM((TQ, 4 * TQ), BF16), pltpu.VMEM((LANES, 4 * TQ), F32)]


def _cumsum_kernel(x_ref, f_ref):
    x = x_ref[0]
    t = x.shape[1]
    lane = lax.broadcasted_iota(jnp.int32, x.shape, 1) % LANES
    s = 1
    while s < LANES:
        x = x + jnp.where(lane >= s, pltpu.roll(x, s, 1), 0.0)
        s *= 2
    carry = jnp.zeros((x.shape[0], 1), F32)
    for c in range(t // LANES):
        blk = x[:, c * LANES:(c + 1) * LANES] + carry
        f_ref[0, :, c * LANES:(c + 1) * LANES] = blk
        carry = blk[:, LANES - 1:LANES]


def _cumsum_rows(lf_t):
    b, h, t = lf_t.shape
    spec = pl.BlockSpec((1, h, t), lambda i: (i, 0, 0))
    return pl.pallas_call(
        _cumsum_kernel, grid=(b,), in_specs=[spec], out_specs=spec,
        out_shape=jax.ShapeDtypeStruct(lf_t.shape, F32),
        compiler_params=_cparams(("arbitrary",)), name="fox_cumsum",
    )(lf_t)


def _fox_prompt_kernel(q_ref, k_ref, v_ref, frow_ref, fkrep_ref, mask_ref, o_ref, vt_ref, st_ref, pt_ref, acc_ref):
    i = pl.program_id(1)

    @pl.when(i == 0)
    def _():
        vt_ref[...] = v_ref[0].astype(BF16)

    qt = jnp.concatenate([_blockdiag_cols(q_ref[0, :LANES, :], 2), _blockdiag_cols(q_ref[0, LANES:, :], 2)], axis=1)
    q_off = pl.multiple_of(i * TQ, TQ)

    def bias_chunk(j, c, s):
        h = HEAD_PERM[c // 2]
        cc = c % 2
        off = pl.multiple_of(j * TQ, TQ)
        fq = frow_ref[0, h:h + 1, pl.ds(q_off + cc * LANES, LANES)]
        fk = fkrep_ref[0, h, pl.ds(off, TQ), :]
        mask = mask_ref[jnp.minimum(i - j, 1), :, cc * LANES:(cc + 1) * LANES]
        return s + (fq - fk) + mask

    l = _flash_t(i, qt, lambda off: k_ref[0, pl.ds(off, TQ), :], lambda off: vt_ref[:, pl.ds(off, TQ)],
                 bias_chunk, st_ref, pt_ref, acc_ref)
    o = acc_ref[...] * (1.0 / l)
    for s in range(2):
        o_ref[0, :, s * LANES:(s + 1) * LANES] = _to_rows(_pair_halves(o, s))


def _fox_prompt(q_t, k, v_t, frow, fkrep, mask_t):
    b, _, t = q_t.shape
    return pl.pallas_call(
        _fox_prompt_kernel,
        grid=(b, t // TQ),
        in_specs=[pl.BlockSpec((1, 256, TQ), lambda bi, i: (bi, 0, i)),
                  pl.BlockSpec((1, t, LANES), lambda bi, i: (bi, 0, 0)),
                  pl.BlockSpec((1, LANES, t), lambda bi, i: (bi, 0, 0)),
                  pl.BlockSpec((1, 8, t), lambda bi, i: (bi, 0, 0)),
                  pl.BlockSpec((1, N_HEADS, t, LANES), lambda bi, i: (bi, 0, 0, 0)),
                  pl.BlockSpec(mask_t.shape, lambda bi, i: (0, 0, 0))],
        out_specs=pl.BlockSpec((1, TQ, 256), lambda bi, i: (bi, i, 0)),
        out_shape=jax.ShapeDtypeStruct((b, t, 256), BF16),
        scratch_shapes=[pltpu.VMEM((LANES, t), BF16)] + _T_SCRATCH(),
        compiler_params=_cparams(("arbitrary", "arbitrary")),
        name="fox_prompt",
    )(q_t, k, v_t, frow, fkrep, mask_t)


def _moba_prompt_kernel(q_ref, qf_ref, k_ref, v_ref, bias_ref, o_ref, kb_ref, vt_ref, km_ref, code_ref,
                        st_ref, pt_ref, acc_ref):
    i = pl.program_id(1)
    t = k_ref.shape[1]
    nb = t // TQ

    @pl.when(i == 0)
    def _():
        kb_ref[...] = k_ref[0].astype(BF16)
        vt_ref[...] = v_ref[0].astype(BF16)
        km_ref[...] = jnp.zeros(km_ref.shape, F32)
        for n in range(nb):
            km_ref[n:n + 1, :] = jnp.mean(k_ref[0, n * TQ:(n + 1) * TQ, :], axis=0, keepdims=True)

    qt = jnp.concatenate([_blockdiag_cols(q_ref[0, :LANES, :], 2), _blockdiag_cols(q_ref[0, LANES:, :], 2)], axis=1)
    qft = jnp.concatenate([_blockdiag_cols(qf_ref[0, :LANES, :], 2), _blockdiag_cols(qf_ref[0, LANES:, :], 2)], axis=1)
    gate = _dot(km_ref[...], qft, precision=lax.Precision.HIGHEST)
    sub = lax.broadcasted_iota(jnp.int32, gate.shape, 0)
    gm = jnp.where(sub < i, gate, NEG_INF)
    rank = jnp.zeros(gate.shape, F32)
    for mth in range(nb):
        row = gm[mth:mth + 1, :]
        beats = (row > gm) | ((row == gm) & (sub > mth))
        rank = rank + jnp.where(beats, 1.0, 0.0)
    keep = ((rank < float(MOBA_TOPK)) & (sub < i)) | (sub >= i)
    code = jnp.where(keep, 0.0, NEG_INF)
    for n in range(8):
        code_ref[n] = code[n:n + 1, :]

    def bias_chunk(j, c, s):
        h = HEAD_PERM[c // 2]
        cc = c % 2
        bias = bias_ref[h, jnp.minimum(i - j, 2), :, cc * LANES:(cc + 1) * LANES]
        return s + bias + code_ref[j, :, c * LANES:(c + 1) * LANES]

    l = _flash_t(i, qt, lambda off: kb_ref[pl.ds(off, TQ), :], lambda off: vt_ref[:, pl.ds(off, TQ)],
                 bias_chunk, st_ref, pt_ref, acc_ref)
    o = acc_ref[...] * (1.0 / l)
    for s in range(2):
        o_ref[0, :, s * LANES:(s + 1) * LANES] = _to_rows(_pair_halves(o, s))


def _moba_prompt(q_t, qf_t, k, v_t, bias_t):
    b, _, t = q_t.shape
    assert t // TQ <= 8
    return pl.pallas_call(
        _moba_prompt_kernel,
        grid=(b, t // TQ),
        in_specs=[pl.BlockSpec((1, 256, TQ), lambda bi, i: (bi, 0, i)),
                  pl.BlockSpec((1, 256, TQ), lambda bi, i: (bi, 0, i)),
                  pl.BlockSpec((1, t, LANES), lambda bi, i: (bi, 0, 0)),
                  pl.BlockSpec((1, LANES, t), lambda bi, i: (bi, 0, 0)),
                  pl.BlockSpec(bias_t.shape, lambda bi, i: (0, 0, 0, 0))],
        out_specs=pl.BlockSpec((1, TQ, 256), lambda bi, i: (bi, i, 0)),
        out_shape=jax.ShapeDtypeStruct((b, t, 256), BF16),
        scratch_shapes=[pltpu.VMEM((t, LANES), BF16), pltpu.VMEM((LANES, t), BF16), pltpu.VMEM((8, LANES), F32),
                        pltpu.VMEM((8, 1, 4 * TQ), F32)] + _T_SCRATCH(),
        compiler_params=_cparams(("arbitrary", "arbitrary")),
        name="moba_prompt",
    )(q_t, qf_t, k, v_t, bias_t)


def _diff_prompt_kernel(q_ref, k_ref, v_ref, bias_ref, cl_ref, g2c_ref, o_ref, vt_ref, st_ref, pt_ref, acc_ref,
                        *, lam_init):
    i = pl.program_id(1)

    @pl.when(i == 0)
    def _():
        vt_ref[...] = v_ref[0].astype(BF16)

    lam = _diff_lambda(cl_ref[...], lam_init)
    sub = lax.broadcasted_iota(jnp.int32, (LANES, TQ), 0)
    lo = sub < LANES // 2
    for sl in range(2):
        rows = slice(sl * LANES, (sl + 1) * LANES)
        qt = _blockdiag_cols(q_ref[0, rows, :], 4)

        def bias_chunk(j, c, s, sl=sl):
            cc = c % 2
            return s + bias_ref[2 * sl + c // 4, jnp.minimum(i - j, 2), :, cc * LANES:(cc + 1) * LANES]

        l = _flash_t(i, qt, lambda off, rows=rows: k_ref[0, pl.ds(off, TQ), rows],
                     lambda off, rows=rows: vt_ref[rows, pl.ds(off, TQ)], bias_chunk, st_ref, pt_ref, acc_ref,
                     scale=C_QK_DIM ** -0.5)
        o = acc_ref[...] * (1.0 / l)
        o_h0 = o[:, :TQ] - lam * o[:, TQ:2 * TQ]
        o_h1 = o[:, 2 * TQ:3 * TQ] - lam * o[:, 3 * TQ:]
        pair = jnp.where(lo, o_h0, o_h1)
        sq = pair * pair
        ss_lo = jnp.sum(jnp.where(lo, sq, 0.0), axis=0, keepdims=True)
        ss_hi = jnp.sum(jnp.where(lo, 0.0, sq), axis=0, keepdims=True)
        ms = jnp.where(lo, ss_lo, ss_hi) * (1.0 / D_V)
        y = (pair * lax.rsqrt(ms + EPS) * g2c_ref[...]) * (1.0 - lam_init)
        o_ref[0, :, rows] = _to_rows(y)


def _diff_prompt(q_t, k, v_t, bias_t, cl, g2c, lam_init):
    b, _, t = q_t.shape
    return pl.pallas_call(
        functools.partial(_diff_prompt_kernel, lam_init=lam_init),
        grid=(b, t // TQ),
        in_specs=[pl.BlockSpec((1, 256, TQ), lambda bi, i: (bi, 0, i)),
                  pl.BlockSpec((1, t, 256), lambda bi, i: (bi, 0, 0)),
                  pl.BlockSpec((1, 256, t), lambda bi, i: (bi, 0, 0)),
                  pl.BlockSpec(bias_t.shape, lambda bi, i: (0, 0, 0, 0)),
                  pl.BlockSpec(cl.shape, lambda bi, i: (0, 0)),
                  pl.BlockSpec(g2c.shape, lambda bi, i: (0, 0))],
        out_specs=pl.BlockSpec((1, TQ, 256), lambda bi, i: (bi, i, 0)),
        out_shape=jax.ShapeDtypeStruct((b, t, 256), BF16),
        scratch_shapes=[pltpu.VMEM((256, t), BF16)] + _T_SCRATCH(),
        compiler_params=_cparams(("arbitrary", "arbitrary")),
        name="diff_prompt",
    )(q_t, k, v_t, bias_t, cl, g2c)


def _mla_prompt_kernel(q_ref, ckv_ref, kr_ref, vt_ref, mask_ref, wuvt_ref, o_ref, kc_ref, st_ref, pt_ref, acc_ref):
    i = pl.program_id(1)

    @pl.when(i == 0)
    def _():
        kc_ref[:, :LANES] = ckv_ref[0].astype(BF16)
        kc_ref[:, LANES:] = kr_ref[0].astype(BF16)

    qt = jnp.concatenate([q_ref[0, h * 256:(h + 1) * 256, :] for h in range(N_HEADS)], axis=1)

    def bias_chunk(j, c, s):
        cc = c % 2
        return s + mask_ref[jnp.minimum(i - j, 1), :, cc * LANES:(cc + 1) * LANES]

    l = _flash_t(i, qt, lambda off: kc_ref[pl.ds(off, TQ), :], lambda off: vt_ref[0, :, pl.ds(off, TQ)],
                 bias_chunk, st_ref, pt_ref, acc_ref, scale=(D_NOPE + D_ROPE) ** -0.5)
    o = (acc_ref[...] * (1.0 / l)).astype(BF16)
    out = _dot(wuvt_ref[0], o[:, :TQ])
    for h in range(1, N_HEADS):
        out = out + _dot(wuvt_ref[h], o[:, h * TQ:(h + 1) * TQ])
    o_ref[0] = _to_rows(out)


def _mla_prompt(qd_t, ckv, kr, ckv_t, mask_t, wuv_t):
    b, _, t = qd_t.shape
    return pl.pallas_call(
        _mla_prompt_kernel,
        grid=(b, t // TQ),
        in_specs=[pl.BlockSpec((1, 1024, TQ), lambda bi, i: (bi, 0, i)),
                  pl.BlockSpec((1, t, LANES), lambda bi, i: (bi, 0, 0)),
                  pl.BlockSpec((1, t, LANES), lambda bi, i: (bi, 0, 0)),
                  pl.BlockSpec((1, LANES, t), lambda bi, i: (bi, 0, 0)),
                  pl.BlockSpec(mask_t.shape, lambda bi, i: (0, 0, 0)),
                  pl.BlockSpec(wuv_t.shape, lambda bi, i: (0, 0, 0))],
        out_specs=pl.BlockSpec((1, TQ, 256), lambda bi, i: (bi, i, 0)),
        out_shape=jax.ShapeDtypeStruct((b, t, 256), BF16),
        scratch_shapes=[pltpu.VMEM((t, 2 * LANES), BF16)] + _T_SCRATCH(),
        compiler_params=_cparams(("arbitrary", "arbitrary")),
        name="mla_prompt",
    )(qd_t, ckv, kr, ckv_t, mask_t, wuv_t)


def _page_specs(layer, tail, n_chunks, new_step, reverse=False):
    P = PAGES_PER_STEP
    specs = []
    for j in range(P):
        def imap(b, c, pt, j=j):
            cl = jnp.maximum(c - 1, 0) if new_step else c
            if reverse:
                cl = n_chunks - 1 - cl
            return (layer, pt[b, cl * P + j]) + (0,) * len(tail)
        specs.append(pl.BlockSpec((None, None) + tail, imap))
    return specs


def _cat_t(refs, sl=None):
    parts = []
    for r in refs:
        x = r[...] if sl is None else r[sl]
        parts.append(x.reshape(-1, x.shape[-1]))
    return jnp.concatenate(parts, axis=1)


def _state_update(m_ref, l_ref, acc_ref, s_blocks, v, first, v_is_t=False):
    rows = acc_ref.shape[0]
    if first:
        m, l, acc = _init_state(rows)
    else:
        m, l, acc = m_ref[...], l_ref[...], acc_ref[...]
    m, l, acc = _softmax_step(s_blocks, m, l, acc, v, v_is_t)
    m_ref[...] = m
    l_ref[...] = l
    acc_ref[...] = acc


def _blockdiag_rows_host(q, n_groups):
    lane = jnp.arange(LANES)
    gw = LANES // n_groups
    parts = [jnp.where((lane >= g * gw) & (lane < (g + 1) * gw), q, jnp.zeros_like(q)) for g in range(n_groups)]
    return jnp.concatenate(parts, axis=1)


def _fox_decode_kernel(pt_ref, q_ref, kn_ref, vn_ref, lfn_ref, own_ref, *rest):
    P = PAGES_PER_STEP
    k_refs, v_refs, lf_refs = rest[:P], rest[P:2 * P], rest[2 * P:3 * P]
    o_ref = rest[3 * P]
    m_ref, l_ref, acc_ref, cq_ref, s_ref = rest[3 * P + 1:]
    c = pl.program_id(1)
    nq = own_ref.shape[0]
    q = q_ref[0]

    @pl.when(c == 0)
    def _():
        x = lfn_ref[0]
        lane = lax.broadcasted_iota(jnp.int32, x.shape, 1)
        s = 1
        while s < nq:
            x = x + jnp.where(lane >= s, pltpu.roll(x, s, 1), 0.0)
            s *= 2
        eye = lax.broadcasted_iota(jnp.int32, (nq, LANES), 0) == lax.broadcasted_iota(jnp.int32, (nq, LANES), 1)
        cols = []
        for h in HEAD_PERM:
            cols.append(jnp.sum(jnp.where(eye, x[h:h + 1, :], 0.0), axis=-1, keepdims=True))
        cq = jnp.concatenate(cols, axis=0)
        cq_ref[...] = cq
        s_ref[...] = jnp.zeros(s_ref.shape, F32)
        sc = _dot_nt(q, kn_ref[0].astype(BF16))
        blocks = []
        for r, h in enumerate(HEAD_PERM):
            blocks.append(sc[r * nq:(r + 1) * nq] + (cq[r * nq:(r + 1) * nq] - x[h:h + 1, :]) + own_ref[...])
        _state_update(m_ref, l_ref, acc_ref, blocks, vn_ref[0].astype(BF16), first=True)

    @pl.when(c > 0)
    def _():
        x = jnp.concatenate([r[...] for r in lf_refs], axis=1)
        w = x.shape[1]
        lane = lax.broadcasted_iota(jnp.int32, x.shape, 1) % LANES
        incl = x
        s = 1
        while s < LANES:
            incl = incl + jnp.where(lane < LANES - s, pltpu.roll(incl, w - s, 1), 0.0)
            s *= 2
        excl = incl - x
        carry = s_ref[...]
        decay = [None] * P
        for j in range(P - 1, -1, -1):
            decay[j] = excl[:, j * LANES:(j + 1) * LANES] + carry
            carry = carry + incl[:, j * LANES:j * LANES + 1]
        s_ref[...] = carry
        d_all = jnp.concatenate(decay, axis=1)
        sc = _dot(q, _cat_t(k_refs).astype(BF16))
        cq = cq_ref[...]
        blocks = [sc[r * nq:(r + 1) * nq] + (cq[r * nq:(r + 1) * nq] + d_all[h:h + 1, :])
                  for r, h in enumerate(HEAD_PERM)]
        _state_update(m_ref, l_ref, acc_ref, blocks, _cat_t(v_refs).astype(BF16), first=False, v_is_t=True)

    @pl.when(c == pl.num_programs(1) - 1)
    def _():
        o_ref[0] = acc_ref[...] / l_ref[...]


def _fox_decode(layer, page_table, q, kn, vn, lfn, own, cache_k, cache_v, cache_lf):
    b, rows, _ = q.shape
    P = PAGES_PER_STEP
    n_chunks = page_table.shape[1] // P
    bmap = lambda bi, c, pt: (bi, 0, 0)
    in_specs = ([pl.BlockSpec((1, rows, LANES), bmap), pl.BlockSpec((1, LANES, LANES), bmap),
                 pl.BlockSpec((1, LANES, LANES), bmap), pl.BlockSpec((1, 8, LANES), bmap),
                 pl.BlockSpec(own.shape, lambda bi, c, pt: (0, 0))]
                + _page_specs(layer, (KV_HEADS, HEAD_DIM, LANES), n_chunks, True, True)
                + _page_specs(layer, (KV_HEADS, HEAD_DIM, LANES), n_chunks, True, True)
                + _page_specs(layer, (N_HEADS, LANES), n_chunks, True, True))
    return pl.pallas_call(
        _fox_decode_kernel,
        grid_spec=pltpu.PrefetchScalarGridSpec(
            num_scalar_prefetch=1, grid=(b, n_chunks + 1), in_specs=in_specs,
            out_specs=pl.BlockSpec((1, rows, LANES), bmap),
            scratch_shapes=[pltpu.VMEM((rows, 1), F32), pltpu.VMEM((rows, 1), F32),
                            pltpu.VMEM((rows, LANES), F32), pltpu.VMEM((rows, 1), F32),
                            pltpu.VMEM((N_HEADS, LANES), F32)]),
        out_shape=jax.ShapeDtypeStruct((b, rows, LANES), F32),
        compiler_params=_cparams(("arbitrary", "arbitrary")),
        name="fox_decode",
    )(page_table, q, kn, vn, lfn, own, *([cache_k] * P), *([cache_v] * P), *([cache_lf] * P))


def _moba_decode_kernel(pt_ref, q_ref, qf_ref, bias_ref, *rest):
    P = PAGES_PER_STEP
    k_refs, v_refs = rest[:P], rest[P:2 * P]
    g_ref, m_ref, l_ref, acc_ref = rest[2 * P:]
    c = pl.program_id(1)
    nq = bias_ref.shape[2]
    q = q_ref[0]
    nbc = P // 2
    last = jnp.where(c == pl.num_programs(1) - 1, 1, 0)
    lane = lax.broadcasted_iota(jnp.int32, (LANES, LANES), 1)
    km = jnp.zeros((LANES, LANES), F32)
    for n in range(nbc):
        kt = _cat_t(k_refs[2 * n:2 * n + 2])
        vt = _cat_t(v_refs[2 * n:2 * n + 2])
        km = jnp.where(lane == n, jnp.sum(kt, axis=1, keepdims=True) * (1.0 / TQ), km)
        s = _dot(q, kt.astype(BF16))
        dd = last if n == nbc - 1 else 0
        blocks = [s[r * nq:(r + 1) * nq] + bias_ref[h, dd] for r, h in enumerate(HEAD_PERM)]
        m, l, acc = _softmax_step(blocks, *_init_state(4 * nq), vt.astype(BF16), v_is_t=True)
        m_ref[0, 0, n] = m
        l_ref[0, 0, n] = l
        acc_ref[0, 0, n] = acc
    g_ref[0, 0] = _dot(qf_ref[0], km, precision=lax.Precision.HIGHEST)


def _moba_decode(layer, page_table, q, qf, bias, cache_k, cache_v):
    b, rows, _ = q.shape
    P = PAGES_PER_STEP
    n_chunks = page_table.shape[1] // P
    nbc = P // 2
    bmap = lambda bi, c, pt: (bi, 0, 0)
    tail = (KV_HEADS, HEAD_DIM, LANES)
    return pl.pallas_call(
        _moba_decode_kernel,
        grid_spec=pltpu.PrefetchScalarGridSpec(
            num_scalar_prefetch=1, grid=(b, n_chunks),
            in_specs=[pl.BlockSpec((1, rows, LANES), bmap), pl.BlockSpec((1, rows, LANES), bmap),
                      pl.BlockSpec(bias.shape, lambda bi, c, pt: (0, 0, 0, 0))]
            + _page_specs(layer, tail, n_chunks, False) + _page_specs(layer, tail, n_chunks, False),
            out_specs=[pl.BlockSpec((1, 1, rows, LANES), lambda bi, c, pt: (bi, c, 0, 0)),
                       pl.BlockSpec((1, 1, nbc, rows, 1), lambda bi, c, pt: (bi, c, 0, 0, 0)),
                       pl.BlockSpec((1, 1, nbc, rows, 1), lambda bi, c, pt: (bi, c, 0, 0, 0)),
                       pl.BlockSpec((1, 1, nbc, rows, LANES), lambda bi, c, pt: (bi, c, 0, 0, 0))]),
        out_shape=[jax.ShapeDtypeStruct((b, n_chunks, rows, LANES), F32),
                   jax.ShapeDtypeStruct((b, n_chunks, nbc, rows, 1), F32),
                   jax.ShapeDtypeStruct((b, n_chunks, nbc, rows, 1), F32),
                   jax.ShapeDtypeStruct((b, n_chunks, nbc, rows, LANES), F32)],
        compiler_params=_cparams(("arbitrary", "arbitrary")),
        name="moba_decode",
    )(page_table, q, qf, bias, *([cache_k] * P), *([cache_v] * P))


def _moba_combine_kernel(g_ref, mx_ref, lx_ref, acc_ref, q_ref, kn_ref, vn_ref, own_ref, o_ref):
    g = g_ref[0]
    rows, nb = g.shape
    nq = own_ref.shape[1]
    lane = lax.broadcasted_iota(jnp.int32, g.shape, 1)
    rank = jnp.zeros(g.shape, F32)
    for mth in range(nb):
        col = g[:, mth:mth + 1]
        beats = (col > g) | ((col == g) & (lane > mth))
        rank = rank + jnp.where(beats, 1.0, 0.0)
    sel = rank < float(MOBA_TOPK)
    s = _dot_nt(q_ref[0], kn_ref[0].astype(BF16))
    blocks = [s[r * nq:(r + 1) * nq] + own_ref[h] for r, h in enumerate(HEAD_PERM)]
    m_o, l_o, acc_o = _softmax_step(blocks, *_init_state(rows), vn_ref[0].astype(BF16))
    mx = mx_ref[0]
    m_all = jnp.maximum(jnp.max(jnp.where(sel, mx, -jnp.inf), axis=-1, keepdims=True), m_o)
    w = jnp.where(sel, jnp.exp(mx - m_all), 0.0)
    w_o = jnp.exp(m_o - m_all)
    den = jnp.sum(w * lx_ref[0], axis=-1, keepdims=True) + w_o * l_o
    num = w_o * acc_o
    for n in range(nb):
        num = num + w[:, n:n + 1] * acc_ref[0, n]
    o_ref[0] = num / den


def _moba_combine(g, mx, lx, acc, q, kn, vn, own):
    b, rows, nb = g.shape
    bmap = lambda bi: (bi, 0, 0)
    return pl.pallas_call(
        _moba_combine_kernel,
        grid=(b,),
        in_specs=[pl.BlockSpec((1, rows, nb), bmap), pl.BlockSpec((1, rows, nb), bmap),
                  pl.BlockSpec((1, rows, nb), bmap),
                  pl.BlockSpec((1, nb, rows, LANES), lambda bi: (bi, 0, 0, 0)),
                  pl.BlockSpec((1, rows, LANES), bmap), pl.BlockSpec((1, LANES, LANES), bmap),
                  pl.BlockSpec((1, LANES, LANES), bmap),
                  pl.BlockSpec(own.shape, lambda bi: (0, 0, 0))],
        out_specs=pl.BlockSpec((1, rows, LANES), bmap),
        out_shape=jax.ShapeDtypeStruct((b, rows, LANES), F32),
        compiler_params=_cparams(("arbitrary",)),
        name="moba_combine",
    )(g, mx, lx, acc, q, kn, vn, own)


def _diff_finish(l, acc, lam, g2, lam_init, t):
    o = acc / l
    o_h0 = o[:t] - lam * o[t:2 * t]
    o_h1 = o[2 * t:3 * t] - lam * o[3 * t:]
    pair = jnp.where(_lane_lo(o_h0.shape), o_h0, o_h1)
    return _subln(pair, g2, lam_init)


def _diff_decode_kernel(pt_ref, q_ref, kn_ref, vn_ref, own_ref, bias_ref, cl_ref, g2_ref, *rest, lam_init):
    P = PAGES_PER_STEP
    k_refs, v_refs = rest[:P], rest[P:2 * P]
    o_ref = rest[2 * P]
    m_ref, l_ref, acc_ref = rest[2 * P + 1:]
    c = pl.program_id(1)
    nq = own_ref.shape[1]
    scale = C_QK_DIM ** -0.5
    last = jnp.where(c == pl.num_programs(1) - 1, 1, 0)

    for sl in range(2):
        lanes = slice(sl * LANES, (sl + 1) * LANES)
        hs = slice(2 * sl, 2 * sl + 2)
        q4 = q_ref[0, sl]
        heads = [2 * sl + r // 2 for r in range(4)]

        @pl.when(c == 0)
        def _(lanes=lanes, q4=q4, heads=heads, sl=sl):
            s = _dot_nt(q4, kn_ref[0, :, lanes].astype(BF16)) * scale
            blocks = [s[r * nq:(r + 1) * nq] + own_ref[h] for r, h in enumerate(heads)]
            _state_update(m_ref.at[sl], l_ref.at[sl], acc_ref.at[sl], blocks,
                          vn_ref[0, :, lanes].astype(BF16), first=True)

        @pl.when(c > 0)
        def _(hs=hs, q4=q4, heads=heads, sl=sl):
            s = _dot(q4, _cat_t(k_refs, hs).astype(BF16)) * scale
            blocks = []
            for r, h in enumerate(heads):
                bias = jnp.concatenate([bias_ref[h, 0]] * (P - 1) + [bias_ref[h, last]], axis=1)
                blocks.append(s[r * nq:(r + 1) * nq] + bias)
            _state_update(m_ref.at[sl], l_ref.at[sl], acc_ref.at[sl], blocks,
                          _cat_t(v_refs, hs).astype(BF16), first=False, v_is_t=True)

    @pl.when(c == pl.num_programs(1) - 1)
    def _():
        lam = _diff_lambda(cl_ref[...], lam_init)
        for sl in range(2):
            o_ref[0, :, sl * LANES:(sl + 1) * LANES] = _diff_finish(
                l_ref[sl], acc_ref[sl], lam, g2_ref[...], lam_init, nq).astype(BF16)


def _diff_decode(layer, page_table, q, kn, vn, own, bias, cl, g2, cache_k, cache_v, lam_init):
    b, _, rows, _ = q.shape
    nq = rows // 4
    P = PAGES_PER_STEP
    n_chunks = page_table.shape[1] // P
    bmap = lambda bi, c, pt: (bi, 0, 0)
    tail = (N_HEADS, HEAD_DIM, LANES)
    return pl.pallas_call(
        functools.partial(_diff_decode_kernel, lam_init=lam_init),
        grid_spec=pltpu.PrefetchScalarGridSpec(
            num_scalar_prefetch=1, grid=(b, n_chunks + 1),
            in_specs=[pl.BlockSpec((1, 2, rows, LANES), lambda bi, c, pt: (bi, 0, 0, 0)),
                      pl.BlockSpec((1, LANES, 256), bmap), pl.BlockSpec((1, LANES, 256), bmap),
                      pl.BlockSpec(own.shape, lambda bi, c, pt: (0, 0, 0)),
                      pl.BlockSpec(bias.shape, lambda bi, c, pt: (0, 0, 0, 0)),
                      pl.BlockSpec(cl.shape, lambda bi, c, pt: (0, 0)),
                      pl.BlockSpec(g2.shape, lambda bi, c, pt: (0, 0))]
            + _page_specs(layer, tail, n_chunks, True) + _page_specs(layer, tail, n_chunks, True),
            out_specs=pl.BlockSpec((1, nq, 256), bmap),
            scratch_shapes=[pltpu.VMEM((2, rows, 1), F32), pltpu.VMEM((2, rows, 1), F32),
                            pltpu.VMEM((2, rows, LANES), F32)]),
        out_shape=jax.ShapeDtypeStruct((b, nq, 256), BF16),
        compiler_params=_cparams(("arbitrary", "arbitrary")),
        name="diff_decode",
    )(page_table, q, kn, vn, own, bias, cl, g2, *([cache_k] * P), *([cache_v] * P))


def _mla_out(o, wuv_ref, t):
    out = _dot(o[:t].astype(BF16), wuv_ref[0])
    for h in range(1, N_HEADS):
        out = out + _dot(o[h * t:(h + 1) * t].astype(BF16), wuv_ref[h])
    return out


def _mla_decode_kernel(pt_ref, ql_ref, qr_ref, cn_ref, rn_ref, own_ref, wuv_ref, *rest):
    P = PAGES_PER_STEP
    c_refs, r_refs = rest[:P], rest[P:2 * P]
    o_ref = rest[2 * P]
    m_ref, l_ref, acc_ref = rest[2 * P + 1:]
    c = pl.program_id(1)
    nq = own_ref.shape[0]
    scale = (D_NOPE + D_ROPE) ** -0.5
    ql = ql_ref[0]
    qr = qr_ref[0]

    @pl.when(c == 0)
    def _():
        ckv = cn_ref[0].astype(BF16)
        s = (_dot_nt(ql, ckv) + _dot_nt(qr, rn_ref[0].astype(BF16))) * scale
        blocks = [s[r * nq:(r + 1) * nq] + own_ref[...] for r in range(N_HEADS)]
        _state_update(m_ref, l_ref, acc_ref, blocks, ckv, first=True)

    @pl.when(c > 0)
    def _():
        ckv = jnp.concatenate([r[...] for r in c_refs], axis=0).astype(BF16)
        krt = jnp.concatenate([r[...] for r in r_refs], axis=1).astype(BF16)
        s = (_dot_nt(ql, ckv) + _dot(qr, krt)) * scale
        blocks = [s[r * nq:(r + 1) * nq] for r in range(N_HEADS)]
        _state_update(m_ref, l_ref, acc_ref, blocks, ckv, first=False)

    @pl.when(c == pl.num_programs(1) - 1)
    def _():
        o_ref[0] = _mla_out(acc_ref[...] / l_ref[...], wuv_ref, nq).astype(BF16)


def _mla_decode(layer, page_table, ql, qr, cn, rn, own, wuv, cache_ckv, cache_kr):
    b, rows, _ = ql.shape
    nq = rows // N_HEADS
    P = PAGES_PER_STEP
    n_chunks = page_table.shape[1] // P
    bmap = lambda bi, c, pt: (bi, 0, 0)
    return pl.pallas_call(
        _mla_decode_kernel,
        grid_spec=pltpu.PrefetchScalarGridSpec(
            num_scalar_prefetch=1, grid=(b, n_chunks + 1),
            in_specs=[pl.BlockSpec((1, rows, LANES), bmap), pl.BlockSpec((1, rows, D_ROPE), bmap),
                      pl.BlockSpec((1, LANES, LANES), bmap), pl.BlockSpec((1, LANES, D_ROPE), bmap),
                      pl.BlockSpec(own.shape, lambda bi, c, pt: (0, 0)),
                      pl.BlockSpec(wuv.shape, lambda bi, c, pt: (0, 0, 0))]
            + _page_specs(layer, (LANES, LANES), n_chunks, True)
            + _page_specs(layer, (D_ROPE, LANES), n_chunks, True),
            out_specs=pl.BlockSpec((1, nq, 256), bmap),
            scratch_shapes=[pltpu.VMEM((rows, 1), F32), pltpu.VMEM((rows, 1), F32),
                            pltpu.VMEM((rows, LANES), F32)]),
        out_shape=jax.ShapeDtypeStruct((b, nq, 256), BF16),
        compiler_params=_cparams(("arbitrary", "arbitrary")),
        name="mla_decode",
    )(page_table, ql, qr, cn, rn, own, wuv, *([cache_ckv] * P), *([cache_kr] * P))


def _bucket_table():
    d = np.arange(MAX_DISTANCE + 1)
    max_exact = N_BUCKETS // 2
    df = np.maximum(d, 1).astype(np.float32)
    large = max_exact + (np.log(df / max_exact) / math.log(MAX_DISTANCE / max_exact)
                         * (N_BUCKETS - max_exact)).astype(np.int32)
    large = np.minimum(large, N_BUCKETS - 1)
    return np.where(d < max_exact, d, large)


def _tables(rel_bias, t, nq, q0, bs):
    bd = rel_bias[_bucket_table()].T
    r = np.arange(TQ)[:, None]
    c = np.arange(TQ)[None, :]
    idx0 = np.clip(r - c, 0, MAX_DISTANCE)
    idx1 = np.clip(TQ + r - c, 0, MAX_DISTANCE)
    far = jnp.broadcast_to(bd[:, MAX_DISTANCE][:, None, None], (bd.shape[0], TQ, TQ))
    prompt = jnp.stack([jnp.where(r >= c, bd[:, idx0], NEG_INF), bd[:, idx1], far], axis=1)
    mask2 = jnp.stack([jnp.where(r >= c, 0.0, NEG_INF).astype(F32), jnp.zeros((TQ, TQ), F32)])
    tq = np.arange(nq)[:, None]
    u = np.arange(LANES)[None, :]
    own_ok = (u <= tq) & (u < nq)
    own = jnp.where(own_ok, bd[:, np.clip(tq - u, 0, MAX_DISTANCE)], NEG_INF)
    own_mask = jnp.where(own_ok, 0.0, NEG_INF).astype(F32)
    idx_last = np.clip(LANES + tq - u, 0, MAX_DISTANCE)
    page_far = jnp.broadcast_to(bd[:, MAX_DISTANCE][:, None, None], (bd.shape[0], nq, LANES))
    page = jnp.stack([page_far, bd[:, idx_last]], axis=1)
    blk = jnp.stack([jnp.concatenate([page_far, page_far], axis=-1),
                     jnp.concatenate([page_far, bd[:, idx_last]], axis=-1)], axis=1)
    cos_p, sin_p = _rope_tables(jnp.arange(t))
    cos_s, sin_s = _rope_tables(q0 + jnp.arange(nq))
    return dict(prompt_t=jnp.swapaxes(prompt, -1, -2), mask_t=jnp.swapaxes(mask2, -1, -2),
                own=own, own_mask=own_mask, page=page, blk=blk,
                cos_p=cos_p, sin_p=sin_p, cos_pt=cos_p.T, sin_pt=sin_p.T,
                cos_s=jnp.tile(cos_s, (bs, 1)), sin_s=jnp.tile(sin_s, (bs, 1)))


def _rope_tables(pos):
    half = D_ROPE // 2
    inv = jnp.power(ROPE_BASE, -jnp.arange(half, dtype=F32) / half)
    ang = pos.astype(F32)[:, None] * inv
    c, s = jnp.cos(ang), jnp.sin(ang)
    pad = jnp.zeros((pos.shape[0], LANES - D_ROPE), F32)
    return jnp.concatenate([c, c, pad], axis=1), jnp.concatenate([-s, s, pad], axis=1)


def _perm_heads(w, axis):
    parts = jnp.split(w, N_HEADS, axis=axis)
    return jnp.concatenate([parts[h] for h in HEAD_PERM], axis=axis)


def _layer_weights(l, norm_g, w_in, b_forget, b_gate, d_q_norm_g, d_w_q_up, d_kv_norm_g, d_w_kv_up,
                   c_subln_g, w_branch, w_out):
    w = w_in[l]
    d = w.shape[0]
    splits = (256, 128, 128, 4, 256, 256, 128, 128, 256, 256, 256, 256, 256, 256, 128, 32, 256, 4 * d)
    offs = np.cumsum((0,) + splits)
    (a_q, a_k, a_v, a_f, a_z, b_q, b_k, b_v, b_z, c_q, c_k, c_v, c_z, d_qa, d_kva, d_kr, d_z, gates) = [
        w[:, offs[i]:offs[i + 1]] for i in range(len(splits))]
    scale = HEAD_DIM ** -0.5
    zpad = lambda x, n: jnp.concatenate([x, jnp.zeros((d, n - x.shape[1]), x.dtype)], axis=1)
    swap = jnp.concatenate([d_kr[:, D_ROPE // 2:], d_kr[:, :D_ROPE // 2]], axis=1)
    segs = dict(qa=_perm_heads(a_q, 1) * scale, ka=a_k, va=a_v, qb=_perm_heads(b_q, 1) * scale, kb=b_k, vb=b_v,
                qc=c_q, kc=c_k, vc=c_v, dqa=d_qa, dkva=d_kva, kr=zpad(d_kr, LANES), krs=zpad(swap, LANES),
                z=jnp.concatenate([_perm_heads(a_z, 1), _perm_heads(b_z, 1), c_z, d_z], axis=1))
    w_pack = jnp.concatenate([zpad(a_f, LANES) if n == "af" else segs[n] for n, _ in _SEGS], axis=1).astype(BF16)
    w_rows = jnp.concatenate([segs[n] for n, _ in _SEGS_R], axis=1).astype(BF16)
    w_t = jnp.concatenate([zpad(a_f, 8) if n == "af" else segs[n] for n, _ in _SEGS_T], axis=1).astype(BF16).T
    wq = d_w_q_up[l]
    r = wq.shape[0]
    nope = wq[:, :, :D_NOPE].reshape(r, N_HEADS * D_NOPE)
    rope = wq[:, :, D_NOPE:]
    rope_sw = jnp.concatenate([rope[..., D_ROPE // 2:], rope[..., :D_ROPE // 2]], axis=-1)

    def spread(x):
        z1 = jnp.zeros((r, N_HEADS, D_LAT), x.dtype)
        z2 = jnp.zeros((r, N_HEADS, 256 - D_LAT - D_ROPE), x.dtype)
        return jnp.concatenate([z1, x, z2], axis=-1).reshape(r, N_HEADS * 256)

    wq_pack = jnp.concatenate([nope, spread(rope), spread(rope_sw)], axis=1).astype(BF16)
    wkv = d_w_kv_up[l]
    w_uk = wkv[:, :, :D_NOPE]
    w_uv = wkv[:, :, D_NOPE:]
    wuk = jnp.zeros((N_HEADS, D_NOPE, N_HEADS, 256), F32)
    wuv = jnp.zeros((N_HEADS, D_LAT, N_HEADS, D_V), F32)
    for h in range(N_HEADS):
        wuk = wuk.at[h, :, h, :D_LAT].set(w_uk[:, h, :].T)
        wuv = wuv.at[h, :, h, :].set(w_uv[:, h, :])
    wuk = wuk.reshape(N_HEADS * D_NOPE, N_HEADS * 256).astype(BF16)
    wuv = wuv.reshape(N_HEADS, D_LAT, N_HEADS * D_V).astype(BF16)
    wbr = w_branch[l]
    wbr = jnp.stack([_perm_heads(wbr[0], 0), _perm_heads(wbr[1], 0), wbr[2], wbr[3]])
    bf = jnp.concatenate([b_forget[l].astype(F32), jnp.zeros((LANES - N_HEADS,), F32)])
    g2 = jnp.concatenate([c_subln_g[l], c_subln_g[l]])
    return dict(
        norm_g=norm_g[l][None, :], w_pack=w_pack, w_rows=w_rows, w_t=w_t,
        b_forget=bf[None, :], bf_col=bf[:8, None],
        gq=d_q_norm_g[l][None, :], gq_col=d_q_norm_g[l][:, None],
        gkv=d_kv_norm_g[l][None, :], gkv_col=d_kv_norm_g[l][:, None],
        wq=wq_pack, wq_t=wq_pack.T, wuk=wuk, wuk_t=wuk.T, wuv=wuv, wuv_t=jnp.swapaxes(wuv, 1, 2),
        wg=gates.astype(BF16), b_gate=b_gate[l][None, :], wbr=wbr.astype(BF16), wout=w_out[l].astype(BF16),
        g2=g2[None, :], g2_col=g2[:, None])


def _unbd(o, nq):
    lo = jnp.arange(LANES) < LANES // 2
    left = jnp.where(lo, o[:, :nq], o[:, nq:2 * nq])
    right = jnp.where(lo, o[:, 2 * nq:3 * nq], o[:, 3 * nq:])
    return jnp.concatenate([left, right], axis=-1).reshape(-1, 256)


def _pad_page(x):
    return jnp.pad(x, ((0, 0), (0, LANES - x.shape[1]), (0, 0)))


def kernel(x_prompt, x_sample, cache_a_k, cache_a_v, cache_a_logf, cache_b_k, cache_b_v, cache_c_k, cache_c_v, cache_d_ckv, cache_d_kr, page_table, norm_g, w_in, b_forget, b_gate, d_q_norm_g, d_w_q_up, d_kv_norm_g, d_w_kv_up, c_lambda, c_subln_g, w_branch, w_out, rel_bias, final_norm_g):
    bp, t, d = x_prompt.shape
    bs, nq, _ = x_sample.shape
    depth, n_phys, page = cache_a_k.shape[:3]
    n_pages = page_table.shape[1]
    q0 = n_pages * page
    assert page == LANES and t % TQ == 0 and n_pages % PAGES_PER_STEP == 0 and nq % 8 == 0

    t5 = lambda c: jnp.transpose(c, (0, 1, 3, 4, 2))
    ca_k, ca_v, cb_k, cb_v, cc_k, cc_v = map(t5, (cache_a_k, cache_a_v, cache_b_k, cache_b_v, cache_c_k, cache_c_v))
    ca_lf = jnp.swapaxes(cache_a_logf, 2, 3)
    cd_kr = jnp.swapaxes(cache_d_kr, 2, 3)

    tabs = _tables(rel_bias.astype(F32), t, nq, q0, bs)
    tm_s = bs * nq
    final_g = final_norm_g[None, :]
    rows = 4 * nq
    nbc = PAGES_PER_STEP // 2

    hp = x_prompt.reshape(bp * t, d)
    hs = x_sample.reshape(bs * nq, d)
    rows_p, rows_s = [], []
    for l in range(depth):
        lam_init = 0.8 - 0.6 * math.exp(-0.3 * l)
        lw = _layer_weights(l, norm_g, w_in, b_forget, b_gate, d_q_norm_g, d_w_q_up, d_kv_norm_g, d_w_kv_up,
                            c_subln_g, w_branch, w_out)
        cl = c_lambda[l].astype(F32)
        last = l == depth - 1

        pr = _inproj_prompt(hp, lw, tabs, bp, t, 256)
        r3 = lambda a: a.reshape(bp, t, a.shape[-1])
        frow = _cumsum_rows(pr["lfT"])
        fkrep = jnp.broadcast_to(frow[:, :N_HEADS, :, None], (bp, N_HEADS, t, LANES))
        o_a = _fox_prompt(pr["qaT"], r3(pr["ka"]), pr["vaT"], frow, fkrep, tabs["mask_t"])
        o_b = _moba_prompt(pr["qbT"], pr["qbfT"], r3(pr["kb"]), pr["vbT"], tabs["prompt_t"][:N_HEADS])
        o_c = _diff_prompt(pr["qcT"], r3(pr["kc"]), pr["vcT"], tabs["prompt_t"][N_HEADS:], cl, lw["g2_col"], lam_init)
        o_d = _mla_prompt(pr["qdT"], r3(pr["ckv"]), r3(pr["kr"]), pr["ckvT"], tabs["mask_t"], lw["wuv_t"])
        f2 = lambda a: a.reshape(bp * t, a.shape[-1])
        hp = _merge(hp, f2(o_a), f2(o_b), f2(o_c), f2(o_d), pr["z"], lw, final_g, last, 512)
        heads_t = lambda a, nh: jnp.transpose(a.reshape(bp, nh, -1, t), (0, 3, 1, 2))
        rows_p.append((heads_t(pr["kaT"], KV_HEADS), heads_t(pr["vaT"], KV_HEADS),
                       jnp.swapaxes(pr["lfT"][:, :N_HEADS], 1, 2),
                       heads_t(pr["kbT"], KV_HEADS), heads_t(pr["vbT"], KV_HEADS),
                       heads_t(pr["kcT"], N_HEADS), heads_t(pr["vcT"], N_HEADS),
                       r3(pr["ckv"]), jnp.swapaxes(pr["krT"][:, :D_ROPE], 1, 2)))

        sr = _inproj(hs, lw, tabs["cos_s"], tabs["sin_s"], tm_s)
        s3 = lambda a: a.reshape(bs, nq, a.shape[-1])
        lf_s = s3(sr["lf"])[:, :, :N_HEADS]
        qa = s3(sr["qa"])
        q_bd = jnp.concatenate([_blockdiag_rows_host(qa[..., :LANES], 2), _blockdiag_rows_host(qa[..., LANES:], 2)], axis=1)
        lfn = jnp.pad(jnp.swapaxes(lf_s, 1, 2), ((0, 0), (0, 8 - N_HEADS), (0, LANES - nq)))
        o = _fox_decode(l, page_table, q_bd, _pad_page(s3(sr["ka"])), _pad_page(s3(sr["va"])), lfn,
                        tabs["own_mask"], ca_k, ca_v, ca_lf)
        o_a = _unbd(o, nq).astype(BF16)
        qb = s3(sr["qb"])
        qbf = s3(sr["qbf"])
        q_bd = jnp.concatenate([_blockdiag_rows_host(qb[..., :LANES], 2), _blockdiag_rows_host(qb[..., LANES:], 2)], axis=1)
        qf_bd = jnp.concatenate([_blockdiag_rows_host(qbf[..., :LANES], 2), _blockdiag_rows_host(qbf[..., LANES:], 2)], axis=1)
        g, mx, lx, acc = _moba_decode(l, page_table, q_bd, qf_bd, tabs["blk"][:N_HEADS], cb_k, cb_v)
        g = jnp.swapaxes(g[..., :nbc], 1, 2).reshape(bs, rows, -1)
        mx = jnp.swapaxes(mx.reshape(bs, -1, rows), 1, 2)
        lx = jnp.swapaxes(lx.reshape(bs, -1, rows), 1, 2)
        acc = acc.reshape(bs, -1, rows, LANES)
        o = _moba_combine(g, mx, lx, acc, q_bd, _pad_page(s3(sr["kb"])), _pad_page(s3(sr["vb"])),
                          tabs["own"][:N_HEADS])
        o_b = _unbd(o, nq).astype(BF16)
        qc = s3(sr["qc"])
        q4 = jnp.stack([_blockdiag_rows_host(qc[..., :LANES], 4), _blockdiag_rows_host(qc[..., LANES:], 4)], axis=1)
        o_c = _diff_decode(l, page_table, q4, _pad_page(s3(sr["kc"])), _pad_page(s3(sr["vc"])),
                           tabs["own"][N_HEADS:], tabs["page"][N_HEADS:], cl, lw["g2"], cc_k, cc_v, lam_init)
        o_c = o_c.reshape(bs * nq, 256)
        qd = jnp.swapaxes(s3(sr["qd"]).reshape(bs, nq, N_HEADS, 256), 1, 2).reshape(bs, rows, 256)
        o_d = _mla_decode(l, page_table, qd[..., :D_LAT], qd[..., D_LAT:D_LAT + D_ROPE],
                          _pad_page(s3(sr["ckv"])), _pad_page(s3(sr["kr"])[..., :D_ROPE]),
                          tabs["own_mask"], lw["wuv"], cache_d_ckv, cd_kr)
        o_d = o_d.reshape(bs * nq, 256)
        hs = _merge(hs, o_a, o_b, o_c, o_d, sr["z"], lw, final_g, last, tm_s)
        rows_s.append((sr["ka"].reshape(bs, nq, KV_HEADS, HEAD_DIM), sr["va"].reshape(bs, nq, KV_HEADS, HEAD_DIM), lf_s,
                       sr["kb"].reshape(bs, nq, KV_HEADS, HEAD_DIM), sr["vb"].reshape(bs, nq, KV_HEADS, HEAD_DIM),
                       sr["kc"].reshape(bs, nq, N_HEADS, 2 * C_QK_DIM), sr["vc"].reshape(bs, nq, N_HEADS, HEAD_DIM),
                       s3(sr["ckv"]), s3(sr["kr"])[:, :, :D_ROPE]))

    stack = lambda rows, i: jnp.stack([r[i] for r in rows], axis=0)
    return ((hp.reshape(bp, t, d), hs.reshape(bs, nq, d))
            + tuple(stack(rows_p, i) for i in range(9)) + tuple(stack(rows_s, i) for i in range(9)))
```

```python
import functools
import math

import jax
import jax.numpy as jnp
import numpy as np
from jax import lax
from jax.experimental import pallas as pl
from jax.experimental.pallas import tpu as pltpu

F32 = jnp.float32
BF16 = jnp.bfloat16

HEAD_DIM = 64
N_HEADS = 4
KV_HEADS = 2
MOBA_TOPK = 3
C_QK_DIM = 32
D_NOPE = 64
D_ROPE = 32
D_V = 64
D_LAT = 128
ROPE_BASE = 10000.0
N_BUCKETS = 32
MAX_DISTANCE = 128
EPS = 1e-6
NEG_INF = -1e30

LANES = 128
TQ = 256
PAGES_PER_STEP = 16
VMEM_LIMIT = 56 * 1024 * 1024

HEAD_PERM = (0, 2, 1, 3)

_SEGS = (("qa", 256), ("ka", 128), ("va", 128), ("qb", 256), ("kb", 128), ("vb", 128),
         ("qc", 256), ("kc", 256), ("vc", 256), ("dqa", 256), ("dkva", 128),
         ("kr", 128), ("krs", 128), ("af", 128), ("z", 1024))
_SEGS_R = (("ka", 128), ("kb", 128), ("kc", 256), ("dkva", 128), ("kr", 128), ("krs", 128), ("z", 1024))
_SEGS_T = (("qa", 256), ("qb", 256), ("qc", 256), ("ka", 128), ("va", 128), ("kb", 128), ("vb", 128),
           ("kc", 256), ("vc", 256), ("dqa", 256), ("dkva", 128), ("kr", 128), ("krs", 128), ("af", 8))


def _offsets(segs):
    off, o = {}, 0
    for n, w in segs:
        off[n] = (o, o + w)
        o += w
    return off


_OFF = _offsets(_SEGS)
_OFF_R = _offsets(_SEGS_R)
_OFF_T = _offsets(_SEGS_T)


def _cparams(sem):
    return pltpu.CompilerParams(dimension_semantics=sem, vmem_limit_bytes=VMEM_LIMIT)


def _dot(a, b, precision=None):
    return jnp.dot(a, b, preferred_element_type=F32, precision=precision)


def _dot_nt(a, b, precision=None):
    return lax.dot_general(a, b, (((1,), (1,)), ((), ())), preferred_element_type=F32,
                           precision=precision)


def _rms(x, g):
    return x * lax.rsqrt(jnp.mean(x * x, axis=-1, keepdims=True) + EPS) * g


def _rms_t(x, g_col):
    return x * lax.rsqrt(jnp.mean(x * x, axis=0, keepdims=True) + EPS) * g_col


def _log_sigmoid(x):
    return jnp.minimum(x, 0.0) - jnp.log(1.0 + jnp.exp(-jnp.abs(x)))


def _inproj_kernel(x_ref, g_ref, w_ref, bf_ref, gq_ref, gkv_ref, wq_ref, wuk_ref, cos_ref, sin_ref,
                   qa_ref, ka_ref, va_ref, lf_ref, qb_ref, qbf_ref, kb_ref, vb_ref,
                   qc_ref, kc_ref, vc_ref, qd_ref, ckv_ref, kr_ref, z_ref):
    hb = _rms(x_ref[...], g_ref[...]).astype(BF16)
    proj = _dot(hb, w_ref[...])

    def seg(name):
        lo, hi = _OFF[name]
        return proj[:, lo:hi]

    qa_ref[...] = seg("qa").astype(BF16)
    ka_ref[...] = seg("ka")
    va_ref[...] = seg("va")
    lf_ref[...] = _log_sigmoid(seg("af") + bf_ref[...])
    qb = seg("qb")
    qb_ref[...] = qb.astype(BF16)
    qbf_ref[...] = qb
    kb_ref[...] = seg("kb")
    vb_ref[...] = seg("vb")
    qc_ref[...] = seg("qc").astype(BF16)
    kc_ref[...] = seg("kc")
    vc_ref[...] = seg("vc")
    z_ref[...] = seg("z").astype(BF16)
    ckv_ref[...] = _rms(seg("dkva"), gkv_ref[...])
    cos = cos_ref[...]
    sin = sin_ref[...]
    kr_ref[...] = seg("kr") * cos + seg("krs") * sin
    qn = _rms(seg("dqa"), gq_ref[...]).astype(BF16)
    qq = _dot(qn, wq_ref[...])
    q_lat = _dot(qq[:, :256].astype(BF16), wuk_ref[...])
    cos8 = jnp.concatenate([cos] * 8, axis=1)
    sin8 = jnp.concatenate([sin] * 8, axis=1)
    q_rope = qq[:, 256:1280] * cos8 + qq[:, 1280:2304] * sin8
    qd_ref[...] = (q_lat + q_rope).astype(BF16)


def _inproj(x, lw, cos_t, sin_t, tm):
    m = x.shape[0]
    row = lambda w: pl.BlockSpec((tm, w), lambda i: (i, 0))
    full = lambda a: pl.BlockSpec(a.shape, lambda i: (0,) * a.ndim)
    outs = (("qa", 256, BF16), ("ka", 128, F32), ("va", 128, F32), ("lf", 128, F32),
            ("qb", 256, BF16), ("qbf", 256, F32), ("kb", 128, F32), ("vb", 128, F32),
            ("qc", 256, BF16), ("kc", 256, F32), ("vc", 256, F32), ("qd", 1024, BF16),
            ("ckv", 128, F32), ("kr", 128, F32), ("z", 1024, BF16))
    ins = (x, lw["norm_g"], lw["w_pack"], lw["b_forget"], lw["gq"], lw["gkv"], lw["wq"], lw["wuk"], cos_t, sin_t)
    res = pl.pallas_call(
        _inproj_kernel,
        grid=(m // tm,),
        in_specs=[row(x.shape[1])] + [full(a) for a in ins[1:8]] + [row(LANES), row(LANES)],
        out_specs=[row(w) for _, w, _ in outs],
        out_shape=[jax.ShapeDtypeStruct((m, w), dt) for _, w, dt in outs],
        compiler_params=_cparams(("arbitrary",)),
        name="inproj",
    )(*ins)
    return {n: r for (n, _, _), r in zip(outs, res)}


_P_ROW_OUTS = (("ka", 128, BF16), ("kb", 128, F32), ("kc", 256, BF16), ("ckv", 128, F32), ("kr", 128, F32),
               ("z", 1024, BF16))
_P_T_OUTS = (("qaT", 256, BF16), ("qbT", 256, BF16), ("qbfT", 256, F32), ("qcT", 256, BF16), ("qdT", 1024, BF16),
             ("kaT", 128, F32), ("vaT", 128, F32), ("kbT", 128, F32), ("vbT", 128, F32), ("kcT", 256, F32),
             ("vcT", 256, F32), ("ckvT", 128, BF16), ("krT", 128, F32), ("lfT", 8, F32))


def _inproj_prompt_kernel(x_ref, g_ref, wr_ref, wt_ref, bfc_ref, gqc_ref, gkv_ref, gkvc_ref, wqt_ref, wukt_ref,
                          cos_ref, sin_ref, cost_ref, sint_ref, *outs):
    o = {n: r for (n, _, _), r in zip(_P_ROW_OUTS + _P_T_OUTS, outs)}
    hb = _rms(x_ref[...], g_ref[...]).astype(BF16)
    pr = _dot(hb, wr_ref[...])

    def seg(name):
        lo, hi = _OFF_R[name]
        return pr[:, lo:hi]

    o["ka"][...] = seg("ka").astype(BF16)
    o["kb"][...] = seg("kb")
    o["kc"][...] = seg("kc").astype(BF16)
    o["ckv"][...] = _rms(seg("dkva"), gkv_ref[...])
    o["kr"][...] = seg("kr") * cos_ref[...] + seg("krs") * sin_ref[...]
    o["z"][...] = seg("z").astype(BF16)

    pt = _dot_nt(wt_ref[...], hb)

    def segt(name):
        lo, hi = _OFF_T[name]
        return pt[lo:hi]

    o["qaT"][0] = segt("qa").astype(BF16)
    qb = segt("qb")
    o["qbT"][0] = qb.astype(BF16)
    o["qbfT"][0] = qb
    o["qcT"][0] = segt("qc").astype(BF16)
    for n in ("ka", "va", "kb", "vb", "kc", "vc"):
        o[n + "T"][0] = segt(n)
    o["lfT"][0] = _log_sigmoid(segt("af") + bfc_ref[...])
    cost = cost_ref[...]
    sint = sint_ref[...]
    o["krT"][0] = segt("kr") * cost + segt("krs") * sint
    o["ckvT"][0] = _rms_t(segt("dkva"), gkvc_ref[...]).astype(BF16)
    qn = _rms_t(segt("dqa"), gqc_ref[...]).astype(BF16)
    qq = _dot(wqt_ref[...], qn)
    q_lat = _dot(wukt_ref[...], qq[:256].astype(BF16))
    cos8 = jnp.concatenate([cost] * 8, axis=0)
    sin8 = jnp.concatenate([sint] * 8, axis=0)
    o["qdT"][0] = (q_lat + qq[256:1280] * cos8 + qq[1280:2304] * sin8).astype(BF16)


def _inproj_prompt(x, lw, tabs, b, t, tm):
    m = x.shape[0]
    n_t = t // tm
    row = lambda w: pl.BlockSpec((tm, w), lambda i: (i, 0))
    full = lambda a: pl.BlockSpec(a.shape, lambda i: (0,) * a.ndim)
    tr = lambda w: pl.BlockSpec((1, w, tm), lambda i: (i // n_t, 0, i % n_t))
    ins = (x, lw["norm_g"], lw["w_rows"], lw["w_t"], lw["bf_col"], lw["gq_col"], lw["gkv"], lw["gkv_col"],
           lw["wq_t"], lw["wuk_t"], tabs["cos_p"], tabs["sin_p"], tabs["cos_pt"], tabs["sin_pt"])
    res = pl.pallas_call(
        _inproj_prompt_kernel,
        grid=(m // tm,),
        in_specs=[row(x.shape[1])] + [full(a) for a in ins[1:10]]
        + [pl.BlockSpec((tm, LANES), lambda i: (i % n_t, 0))] * 2
        + [pl.BlockSpec((LANES, tm), lambda i: (0, i % n_t))] * 2,
        out_specs=[row(w) for _, w, _ in _P_ROW_OUTS] + [tr(w) for _, w, _ in _P_T_OUTS],
        out_shape=[jax.ShapeDtypeStruct((m, w), dt) for _, w, dt in _P_ROW_OUTS]
        + [jax.ShapeDtypeStruct((b, w, t), dt) for _, w, dt in _P_T_OUTS],
        compiler_params=_cparams(("arbitrary",)),
        name="inproj_prompt",
    )(*ins)
    return {n: r for (n, _, _), r in zip(_P_ROW_OUTS + _P_T_OUTS, res)}


def _merge_kernel(x_ref, oa_ref, ob_ref, oc_ref, od_ref, z_ref, g_ref, wg_ref, bg_ref, wbr_ref,
                  wout_ref, fg_ref, y_ref, *, final):
    x = x_ref[...]
    hb = _rms(x, g_ref[...]).astype(BF16)
    d = x.shape[1]
    acc = jnp.zeros(x.shape, F32)
    for n, o_ref in enumerate((oa_ref, ob_ref, oc_ref, od_ref)):
        w = o_ref.shape[1]
        z = z_ref[:, n * w:(n + 1) * w].astype(F32)
        a = (o_ref[...].astype(F32) * (z * jax.nn.sigmoid(z))).astype(BF16)
        u = _dot(a, wbr_ref[n])
        gate = _dot(hb, wg_ref[:, n * d:(n + 1) * d]) + bg_ref[:, n * d:(n + 1) * d]
        acc = acc + jax.nn.sigmoid(gate) * u
    y = x + _dot(acc.astype(BF16), wout_ref[...])
    if final:
        y = _rms(y, fg_ref[...])
    y_ref[...] = y


def _merge(x, o_a, o_b, o_c, o_d, z, lw, final_g, final, tm):
    m, d = x.shape
    row = lambda w: pl.BlockSpec((tm, w), lambda i: (i, 0))
    full = lambda a: pl.BlockSpec(a.shape, lambda i: (0,) * a.ndim)
    return pl.pallas_call(
        functools.partial(_merge_kernel, final=final),
        grid=(m // tm,),
        in_specs=[row(d), row(256), row(256), row(256), row(256), row(1024), full(lw["norm_g"]),
                  full(lw["wg"]), full(lw["b_gate"]), full(lw["wbr"]), full(lw["wout"]),
                  full(final_g)],
        out_specs=row(d),
        out_shape=jax.ShapeDtypeStruct((m, d), F32),
        compiler_params=_cparams(("arbitrary",)),
        name="merge",
    )(x, o_a, o_b, o_c, o_d, z, lw["norm_g"], lw["wg"], lw["b_gate"], lw["wbr"], lw["wout"],
      final_g)


def _lane_lo(shape):
    return lax.broadcasted_iota(jnp.int32, shape, len(shape) - 1) < (LANES // 2)


def _softmax_step(s_blocks, m, l, acc, v, v_is_t=False):
    t = s_blocks[0].shape[0]
    p_blocks, m_new_blocks, alpha_blocks, l_blocks = [], [], [], []
    for r, s in enumerate(s_blocks):
        m_old = m[r * t:(r + 1) * t]
        m_new = jnp.maximum(m_old, jnp.max(s, axis=-1, keepdims=True))
        p = jnp.exp(s - m_new)
        alpha = jnp.exp(m_old - m_new)
        l_blocks.append(alpha * l[r * t:(r + 1) * t] + jnp.sum(p, axis=-1, keepdims=True))
        p_blocks.append(p.astype(BF16))
        m_new_blocks.append(m_new)
        alpha_blocks.append(alpha)
    p_all = jnp.concatenate(p_blocks, axis=0)
    alpha_all = jnp.concatenate(alpha_blocks, axis=0)
    pv = _dot_nt(p_all, v) if v_is_t else _dot(p_all, v)
    acc = alpha_all * acc + pv
    return jnp.concatenate(m_new_blocks, axis=0), jnp.concatenate(l_blocks, axis=0), acc


def _init_state(rows):
    return (jnp.full((rows, 1), -jnp.inf, F32), jnp.zeros((rows, 1), F32), jnp.zeros((rows, LANES), F32))


def _subln(pair, g2, lam_init):
    lo = _lane_lo(pair.shape)
    sq = pair * pair
    ss_lo = jnp.sum(jnp.where(lo, sq, 0.0), axis=-1, keepdims=True)
    ss_hi = jnp.sum(jnp.where(lo, 0.0, sq), axis=-1, keepdims=True)
    ms = jnp.where(lo, ss_lo, ss_hi) * (1.0 / D_V)
    return (pair * lax.rsqrt(ms + EPS) * g2) * (1.0 - lam_init)


def _diff_lambda(cl, lam_init):
    a = jnp.sum(cl[0:1] * cl[1:2], axis=-1, keepdims=True)
    b = jnp.sum(cl[2:3] * cl[3:4], axis=-1, keepdims=True)
    return jnp.exp(a) - jnp.exp(b) + lam_init


def _sub_mask(x, lo, hi):
    sub = lax.broadcasted_iota(jnp.int32, x.shape, 0)
    return jnp.where((sub >= lo) & (sub < hi), x, jnp.zeros_like(x))


def _blockdiag_cols(x, n_groups):
    gw = x.shape[0] // n_groups
    return jnp.concatenate([_sub_mask(x, g * gw, (g + 1) * gw) for g in range(n_groups)], axis=1)


def _flash_t(i, qt, k_tile, v_tile, bias_chunk, st_ref, pt_ref, acc_ref, scale=None):
    r_all = qt.shape[1]
    acc_ref[...] = jnp.zeros(acc_ref.shape, F32)

    def body(j, carry):
        m, l = carry
        off = pl.multiple_of(j * TQ, TQ)
        st_ref[...] = _dot(k_tile(off), qt)
        ms, ls, alphas = [], [], []
        for c in range(r_all // LANES):
            cs = slice(c * LANES, (c + 1) * LANES)
            s = st_ref[:, cs]
            if scale is not None:
                s = s * scale
            s = bias_chunk(j, c, s)
            m_prev = m[:, cs]
            m_new = jnp.maximum(m_prev, jnp.max(s, axis=0, keepdims=True))
            p = jnp.exp(s - m_new)
            alpha = jnp.exp(m_prev - m_new)
            ls.append(alpha * l[:, cs] + jnp.sum(p, axis=0, keepdims=True))
            pt_ref[:, cs] = p.astype(BF16)
            ms.append(m_new)
            alphas.append(alpha)
        acc_ref[...] = acc_ref[...] * jnp.concatenate(alphas, axis=1) + _dot(v_tile(off), pt_ref[...])
        return jnp.concatenate(ms, axis=1), jnp.concatenate(ls, axis=1)

    init = (jnp.full((1, r_all), -jnp.inf, F32), jnp.zeros((1, r_all), F32))
    _, l = lax.fori_loop(0, i + 1, body, init)
    return l


def _eye_bf16(n):
    return jnp.where(lax.broadcasted_iota(jnp.int32, (n, n), 0) == lax.broadcasted_iota(jnp.int32, (n, n), 1),
                     1.0, 0.0).astype(BF16)


def _to_rows(x_t):
    return _dot_nt(_eye_bf16(x_t.shape[1]), x_t.astype(BF16)).astype(BF16)


def _pair_halves(o, s):
    sub = lax.broadcasted_iota(jnp.int32, (LANES, TQ), 0)
    return jnp.where(sub < LANES // 2, o[:, 2 * s * TQ:(2 * s + 1) * TQ], o[:, (2 * s + 1) * TQ:(2 * s + 2) * TQ])


_T_SCRATCH = lambda: [pltpu.VMEM((TQ, 4 * TQ), F32), pltpu.VMEM((TQ, 4 * TQ), BF16), pltpu.VMEM((LANES, 4 * TQ), F32)]


def _cumsum_kernel(x_ref, f_ref):
    x = x_ref[0]
    t = x.shape[1]
    lane = lax.broadcasted_iota(jnp.int32, x.shape, 1) % LANES
    s = 1
    while s < LANES:
        x = x + jnp.where(lane >= s, pltpu.roll(x, s, 1), 0.0)
        s *= 2
    carry = jnp.zeros((x.shape[0], 1), F32)
    for c in range(t // LANES):
        blk = x[:, c * LANES:(c + 1) * LANES] + carry
        f_ref[0, :, c * LANES:(c + 1) * LANES] = blk
        carry = blk[:, LANES - 1:LANES]


def _cumsum_rows(lf_t):
    b, h, t = lf_t.shape
    spec = pl.BlockSpec((1, h, t), lambda i: (i, 0, 0))
    return pl.pallas_call(
        _cumsum_kernel, grid=(b,), in_specs=[spec], out_specs=spec,
        out_shape=jax.ShapeDtypeStruct(lf_t.shape, F32),
        compiler_params=_cparams(("arbitrary",)), name="fox_cumsum",
    )(lf_t)


def _fox_prompt_kernel(q_ref, k_ref, v_ref, frow_ref, fkrep_ref, mask_ref, o_ref, vt_ref, st_ref, pt_ref, acc_ref):
    i = pl.program_id(1)

    @pl.when(i == 0)
    def _():
        vt_ref[...] = v_ref[0].astype(BF16)

    qt = jnp.concatenate([_blockdiag_cols(q_ref[0, :LANES, :], 2), _blockdiag_cols(q_ref[0, LANES:, :], 2)], axis=1)
    q_off = pl.multiple_of(i * TQ, TQ)

    def bias_chunk(j, c, s):
        h = HEAD_PERM[c // 2]
        cc = c % 2
        off = pl.multiple_of(j * TQ, TQ)
        fq = frow_ref[0, h:h + 1, pl.ds(q_off + cc * LANES, LANES)]
        fk = fkrep_ref[0, h, pl.ds(off, TQ), :]
        mask = mask_ref[jnp.minimum(i - j, 1), :, cc * LANES:(cc + 1) * LANES]
        return s + (fq - fk) + mask

    l = _flash_t(i, qt, lambda off: k_ref[0, pl.ds(off, TQ), :], lambda off: vt_ref[:, pl.ds(off, TQ)],
                 bias_chunk, st_ref, pt_ref, acc_ref)
    o = acc_ref[...] * (1.0 / l)
    for s in range(2):
        o_ref[0, :, s * LANES:(s + 1) * LANES] = _to_rows(_pair_halves(o, s))


def _fox_prompt(q_t, k, v_t, frow, fkrep, mask_t):
    b, _, t = q_t.shape
    return pl.pallas_call(
        _fox_prompt_kernel,
        grid=(b, t // TQ),
        in_specs=[pl.BlockSpec((1, 256, TQ), lambda bi, i: (bi, 0, i)),
                  pl.BlockSpec((1, t, LANES), lambda bi, i: (bi, 0, 0)),
                  pl.BlockSpec((1, LANES, t), lambda bi, i: (bi, 0, 0)),
                  pl.BlockSpec((1, 8, t), lambda bi, i: (bi, 0, 0)),
                  pl.BlockSpec((1, N_HEADS, t, LANES), lambda bi, i: (bi, 0, 0, 0)),
                  pl.BlockSpec(mask_t.shape, lambda bi, i: (0, 0, 0))],
        out_specs=pl.BlockSpec((1, TQ, 256), lambda bi, i: (bi, i, 0)),
        out_shape=jax.ShapeDtypeStruct((b, t, 256), BF16),
        scratch_shapes=[pltpu.VMEM((LANES, t), BF16)] + _T_SCRATCH(),
        compiler_params=_cparams(("arbitrary", "arbitrary")),
        name="fox_prompt",
    )(q_t, k, v_t, frow, fkrep, mask_t)


def _moba_prompt_kernel(q_ref, qf_ref, k_ref, v_ref, bias_ref, o_ref, kb_ref, vt_ref, km_ref, code_ref,
                        st_ref, pt_ref, acc_ref):
    i = pl.program_id(1)
    t = k_ref.shape[1]
    nb = t // TQ

    @pl.when(i == 0)
    def _():
        kb_ref[...] = k_ref[0].astype(BF16)
        vt_ref[...] = v_ref[0].astype(BF16)
        km_ref[...] = jnp.zeros(km_ref.shape, F32)
        for n in range(nb):
            km_ref[n:n + 1, :] = jnp.mean(k_ref[0, n * TQ:(n + 1) * TQ, :], axis=0, keepdims=True)

    qt = jnp.concatenate([_blockdiag_cols(q_ref[0, :LANES, :], 2), _blockdiag_cols(q_ref[0, LANES:, :], 2)], axis=1)
    qft = jnp.concatenate([_blockdiag_cols(qf_ref[0, :LANES, :], 2), _blockdiag_cols(qf_ref[0, LANES:, :], 2)], axis=1)
    gate = _dot(km_ref[...], qft, precision=lax.Precision.HIGHEST)
    sub = lax.broadcasted_iota(jnp.int32, gate.shape, 0)
    gm = jnp.where(sub < i, gate, NEG_INF)
    rank = jnp.zeros(gate.shape, F32)
    for mth in range(nb):
        row = gm[mth:mth + 1, :]
        beats = (row > gm) | ((row == gm) & (sub > mth))
        rank = rank + jnp.where(beats, 1.0, 0.0)
    keep = ((rank < float(MOBA_TOPK)) & (sub < i)) | (sub >= i)
    code = jnp.where(keep, 0.0, NEG_INF)
    for n in range(8):
        code_ref[n] = code[n:n + 1, :]

    def bias_chunk(j, c, s):
        h = HEAD_PERM[c // 2]
        cc = c % 2
        bias = bias_ref[h, jnp.minimum(i - j, 2), :, cc * LANES:(cc + 1) * LANES]
        return s + bias + code_ref[j, :, c * LANES:(c + 1) * LANES]

    l = _flash_t(i, qt, lambda off: kb_ref[pl.ds(off, TQ), :], lambda off: vt_ref[:, pl.ds(off, TQ)],
                 bias_chunk, st_ref, pt_ref, acc_ref)
    o = acc_ref[...] * (1.0 / l)
    for s in range(2):
        o_ref[0, :, s * LANES:(s + 1) * LANES] = _to_rows(_pair_halves(o, s))


def _moba_prompt(q_t, qf_t, k, v_t, bias_t):
    b, _, t = q_t.shape
    assert t // TQ <= 8
    return pl.pallas_call(
        _moba_prompt_kernel,
        grid=(b, t // TQ),
        in_specs=[pl.BlockSpec((1, 256, TQ), lambda bi, i: (bi, 0, i)),
                  pl.BlockSpec((1, 256, TQ), lambda bi, i: (bi, 0, i)),
                  pl.BlockSpec((1, t, LANES), lambda bi, i: (bi, 0, 0)),
                  pl.BlockSpec((1, LANES, t), lambda bi, i: (bi, 0, 0)),
                  pl.BlockSpec(bias_t.shape, lambda bi, i: (0, 0, 0, 0))],
        out_specs=pl.BlockSpec((1, TQ, 256), lambda bi, i: (bi, i, 0)),
        out_shape=jax.ShapeDtypeStruct((b, t, 256), BF16),
        scratch_shapes=[pltpu.VMEM((t, LANES), BF16), pltpu.VMEM((LANES, t), BF16), pltpu.VMEM((8, LANES), F32),
                        pltpu.VMEM((8, 1, 4 * TQ), F32)] + _T_SCRATCH(),
        compiler_params=_cparams(("arbitrary", "arbitrary")),
        name="moba_prompt",
    )(q_t, qf_t, k, v_t, bias_t)


def _diff_prompt_kernel(q_ref, k_ref, v_ref, bias_ref, cl_ref, g2c_ref, o_ref, vt_ref, st_ref, pt_ref, acc_ref,
                        *, lam_init):
    i = pl.program_id(1)

    @pl.when(i == 0)
    def _():
        vt_ref[...] = v_ref[0].astype(BF16)

    lam = _diff_lambda(cl_ref[...], lam_init)
    sub = lax.broadcasted_iota(jnp.int32, (LANES, TQ), 0)
    lo = sub < LANES // 2
    for sl in range(2):
        rows = slice(sl * LANES, (sl + 1) * LANES)
        qt = _blockdiag_cols(q_ref[0, rows, :], 4)

        def bias_chunk(j, c, s, sl=sl):
            cc = c % 2
            return s + bias_ref[2 * sl + c // 4, jnp.minimum(i - j, 2), :, cc * LANES:(cc + 1) * LANES]

        l = _flash_t(i, qt, lambda off, rows=rows: k_ref[0, pl.ds(off, TQ), rows],
                     lambda off, rows=rows: vt_ref[rows, pl.ds(off, TQ)], bias_chunk, st_ref, pt_ref, acc_ref,
                     scale=C_QK_DIM ** -0.5)
        o = acc_ref[...] * (1.0 / l)
        o_h0 = o[:, :TQ] - lam * o[:, TQ:2 * TQ]
        o_h1 = o[:, 2 * TQ:3 * TQ] - lam * o[:, 3 * TQ:]
        pair = jnp.where(lo, o_h0, o_h1)
        sq = pair * pair
        ss_lo = jnp.sum(jnp.where(lo, sq, 0.0), axis=0, keepdims=True)
        ss_hi = jnp.sum(jnp.where(lo, 0.0, sq), axis=0, keepdims=True)
        ms = jnp.where(lo, ss_lo, ss_hi) * (1.0 / D_V)
        y = (pair * lax.rsqrt(ms + EPS) * g2c_ref[...]) * (1.0 - lam_init)
        o_ref[0, :, rows] = _to_rows(y)


def _diff_prompt(q_t, k, v_t, bias_t, cl, g2c, lam_init):
    b, _, t = q_t.shape
    return pl.pallas_call(
        functools.partial(_diff_prompt_kernel, lam_init=lam_init),
        grid=(b, t // TQ),
        in_specs=[pl.BlockSpec((1, 256, TQ), lambda bi, i: (bi, 0, i)),
                  pl.BlockSpec((1, t, 256), lambda bi, i: (bi, 0, 0)),
                  pl.BlockSpec((1, 256, t), lambda bi, i: (bi, 0, 0)),
                  pl.BlockSpec(bias_t.shape, lambda bi, i: (0, 0, 0, 0)),
                  pl.BlockSpec(cl.shape, lambda bi, i: (0, 0)),
                  pl.BlockSpec(g2c.shape, lambda bi, i: (0, 0))],
        out_specs=pl.BlockSpec((1, TQ, 256), lambda bi, i: (bi, i, 0)),
        out_shape=jax.ShapeDtypeStruct((b, t, 256), BF16),
        scratch_shapes=[pltpu.VMEM((256, t), BF16)] + _T_SCRATCH(),
        compiler_params=_cparams(("arbitrary", "arbitrary")),
        name="diff_prompt",
    )(q_t, k, v_t, bias_t, cl, g2c)


def _mla_prompt_kernel(q_ref, ckv_ref, kr_ref, vt_ref, mask_ref, wuvt_ref, o_ref, kc_ref, st_ref, pt_ref, acc_ref):
    i = pl.program_id(1)

    @pl.when(i == 0)
    def _():
        kc_ref[:, :LANES] = ckv_ref[0].astype(BF16)
        kc_ref[:, LANES:] = kr_ref[0].astype(BF16)

    qt = jnp.concatenate([q_ref[0, h * 256:(h + 1) * 256, :] for h in range(N_HEADS)], axis=1)

    def bias_chunk(j, c, s):
        cc = c % 2
        return s + mask_ref[jnp.minimum(i - j, 1), :, cc * LANES:(cc + 1) * LANES]

    l = _flash_t(i, qt, lambda off: kc_ref[pl.ds(off, TQ), :], lambda off: vt_ref[0, :, pl.ds(off, TQ)],
                 bias_chunk, st_ref, pt_ref, acc_ref, scale=(D_NOPE + D_ROPE) ** -0.5)
    o = (acc_ref[...] * (1.0 / l)).astype(BF16)
    out = _dot(wuvt_ref[0], o[:, :TQ])
    for h in range(1, N_HEADS):
        out = out + _dot(wuvt_ref[h], o[:, h * TQ:(h + 1) * TQ])
    o_ref[0] = _to_rows(out)


def _mla_prompt(qd_t, ckv, kr, ckv_t, mask_t, wuv_t):
    b, _, t = qd_t.shape
    return pl.pallas_call(
        _mla_prompt_kernel,
        grid=(b, t // TQ),
        in_specs=[pl.BlockSpec((1, 1024, TQ), lambda bi, i: (bi, 0, i)),
                  pl.BlockSpec((1, t, LANES), lambda bi, i: (bi, 0, 0)),
                  pl.BlockSpec((1, t, LANES), lambda bi, i: (bi, 0, 0)),
                  pl.BlockSpec((1, LANES, t), lambda bi, i: (bi, 0, 0)),
                  pl.BlockSpec(mask_t.shape, lambda bi, i: (0, 0, 0)),
                  pl.BlockSpec(wuv_t.shape, lambda bi, i: (0, 0, 0))],
        out_specs=pl.BlockSpec((1, TQ, 256), lambda bi, i: (bi, i, 0)),
        out_shape=jax.ShapeDtypeStruct((b, t, 256), BF16),
        scratch_shapes=[pltpu.VMEM((t, 2 * LANES), BF16)] + _T_SCRATCH(),
        compiler_params=_cparams(("arbitrary", "arbitrary")),
        name="mla_prompt",
    )(qd_t, ckv, kr, ckv_t, mask_t, wuv_t)


def _page_copy(hbm, buf, sems, layer, page, slot, j, a):
    return pltpu.make_async_copy(hbm.at[layer, page], buf.at[slot, j], sems.at[slot, a])


def _paged_walk(pt_ref, layer, n_seq, n_chunks, reverse, caches, sems, new_rows, past_chunk, finish):
    P = PAGES_PER_STEP
    total = n_seq * n_chunks

    def issue(step, slot):
        b = step // n_chunks
        c = step % n_chunks
        cl = (n_chunks - 1 - c) if reverse else c
        for j in range(P):
            page = pt_ref[b, cl * P + j]
            for a, (hbm, buf) in enumerate(caches):
                _page_copy(hbm, buf, sems, layer, page, slot, j, a).start()

    def wait(slot):
        for j in range(P):
            for a, (hbm, buf) in enumerate(caches):
                _page_copy(hbm, buf, sems, layer, 0, slot, j, a).wait()

    issue(0, 0)

    def body(step, carry):
        slot = step % 2

        @pl.when(step + 1 < total)
        def _():
            issue(step + 1, 1 - slot)

        wait(slot)
        b = step // n_chunks
        c = step % n_chunks

        @pl.when(c == 0)
        def _():
            new_rows(b)

        past_chunk(b, c, slot)

        @pl.when(c == n_chunks - 1)
        def _():
            finish(b)

        return carry

    lax.fori_loop(0, total, body, 0)


def _page_refs(buf, slot):
    return [buf.at[slot, j] for j in range(PAGES_PER_STEP)]


def _decode_call(kernel_fn, name, page_table, small, caches, bufs, state, out_shape):
    vspec = lambda a: pl.BlockSpec(a.shape, lambda i, pt, n=a.ndim: (0,) * n)
    return pl.pallas_call(
        kernel_fn,
        grid_spec=pltpu.PrefetchScalarGridSpec(
            num_scalar_prefetch=1, grid=(1,),
            in_specs=[vspec(a) for a in small] + [pl.BlockSpec(memory_space=pl.ANY)] * len(caches),
            out_specs=pl.BlockSpec(out_shape.shape, lambda i, pt, n=len(out_shape.shape): (0,) * n),
            scratch_shapes=bufs + [pltpu.SemaphoreType.DMA((2, len(caches)))] + state),
        out_shape=out_shape,
        compiler_params=_cparams(("arbitrary",)),
        name=name,
    )(page_table, *small, *caches)


def _cat_t(refs, sl=None):
    parts = []
    for r in refs:
        x = r[...] if sl is None else r[sl]
        parts.append(x.reshape(-1, x.shape[-1]))
    return jnp.concatenate(parts, axis=1)


def _state_update(m_ref, l_ref, acc_ref, s_blocks, v, first, v_is_t=False):
    rows = acc_ref.shape[0]
    if first:
        m, l, acc = _init_state(rows)
    else:
        m, l, acc = m_ref[...], l_ref[...], acc_ref[...]
    m, l, acc = _softmax_step(s_blocks, m, l, acc, v, v_is_t)
    m_ref[...] = m
    l_ref[...] = l
    acc_ref[...] = acc


def _blockdiag_rows_host(q, n_groups):
    lane = jnp.arange(LANES)
    gw = LANES // n_groups
    parts = [jnp.where((lane >= g * gw) & (lane < (g + 1) * gw), q, jnp.zeros_like(q)) for g in range(n_groups)]
    return jnp.concatenate(parts, axis=1)


def _fox_decode_kernel(pt_ref, q_ref, kn_ref, vn_ref, lfn_ref, own_ref, ck_ref, cv_ref, clf_ref, o_ref,
                       kbuf, vbuf, lfbuf, sems, m_ref, l_ref, acc_ref, cq_ref, s_ref, *, layer, n_chunks):
    P = PAGES_PER_STEP
    nq = own_ref.shape[0]

    def new_rows(b):
        q = q_ref[b]
        x = lfn_ref[b]
        lane = lax.broadcasted_iota(jnp.int32, x.shape, 1)
        s = 1
        while s < nq:
            x = x + jnp.where(lane >= s, pltpu.roll(x, s, 1), 0.0)
            s *= 2
        eye = lax.broadcasted_iota(jnp.int32, (nq, LANES), 0) == lax.broadcasted_iota(jnp.int32, (nq, LANES), 1)
        cols = []
        for h in HEAD_PERM:
            cols.append(jnp.sum(jnp.where(eye, x[h:h + 1, :], 0.0), axis=-1, keepdims=True))
        cq = jnp.concatenate(cols, axis=0)
        cq_ref[...] = cq
        s_ref[...] = jnp.zeros(s_ref.shape, F32)
        sc = _dot_nt(q, kn_ref[b].astype(BF16))
        blocks = []
        for r, h in enumerate(HEAD_PERM):
            blocks.append(sc[r * nq:(r + 1) * nq] + (cq[r * nq:(r + 1) * nq] - x[h:h + 1, :]) + own_ref[...])
        _state_update(m_ref, l_ref, acc_ref, blocks, vn_ref[b].astype(BF16), first=True)

    def past_chunk(b, c, slot):
        q = q_ref[b]
        k_refs, v_refs, lf_refs = _page_refs(kbuf, slot), _page_refs(vbuf, slot), _page_refs(lfbuf, slot)
        x = jnp.concatenate([r[...] for r in lf_refs], axis=1)
        w = x.shape[1]
        lane = lax.broadcasted_iota(jnp.int32, x.shape, 1) % LANES
        incl = x
        s = 1
        while s < LANES:
            incl = incl + jnp.where(lane < LANES - s, pltpu.roll(incl, w - s, 1), 0.0)
            s *= 2
        excl = incl - x
        carry = s_ref[...]
        decay = [None] * P
        for j in range(P - 1, -1, -1):
            decay[j] = excl[:, j * LANES:(j + 1) * LANES] + carry
            carry = carry + incl[:, j * LANES:j * LANES + 1]
        s_ref[...] = carry
        d_all = jnp.concatenate(decay, axis=1)
        sc = _dot(q, _cat_t(k_refs).astype(BF16))
        cq = cq_ref[...]
        blocks = [sc[r * nq:(r + 1) * nq] + (cq[r * nq:(r + 1) * nq] + d_all[h:h + 1, :])
                  for r, h in enumerate(HEAD_PERM)]
        _state_update(m_ref, l_ref, acc_ref, blocks, _cat_t(v_refs).astype(BF16), first=False, v_is_t=True)

    def finish(b):
        o_ref[b] = acc_ref[...] / l_ref[...]

    _paged_walk(pt_ref, layer, q_ref.shape[0], n_chunks, True,
                [(ck_ref, kbuf), (cv_ref, vbuf), (clf_ref, lfbuf)], sems, new_rows, past_chunk, finish)


def _fox_decode(layer, page_table, q, kn, vn, lfn, own, cache_k, cache_v, cache_lf):
    b, rows, _ = q.shape
    P = PAGES_PER_STEP
    n_chunks = page_table.shape[1] // P
    return _decode_call(
        functools.partial(_fox_decode_kernel, layer=layer, n_chunks=n_chunks), "fox_decode", page_table,
        [q, kn, vn, lfn, own], [cache_k, cache_v, cache_lf],
        [pltpu.VMEM((2, P, KV_HEADS, HEAD_DIM, LANES), F32), pltpu.VMEM((2, P, KV_HEADS, HEAD_DIM, LANES), F32),
         pltpu.VMEM((2, P, N_HEADS, LANES), F32)],
        [pltpu.VMEM((rows, 1), F32), pltpu.VMEM((rows, 1), F32), pltpu.VMEM((rows, LANES), F32),
         pltpu.VMEM((rows, 1), F32), pltpu.VMEM((N_HEADS, LANES), F32)],
        jax.ShapeDtypeStruct((b, rows, LANES), F32))


def _moba_decode_kernel(pt_ref, q_ref, qf_ref, bias_ref, own_ref, kn_ref, vn_ref, ck_ref, cv_ref, o_ref,
                        kbuf, vbuf, sems, km_ref, mx_ref, lx_ref, acc_ref, *, layer, n_chunks):
    P = PAGES_PER_STEP
    nq = bias_ref.shape[2]
    rows = 4 * nq
    nbc = P // 2
    nb = nbc * n_chunks
    lane_f = lax.broadcasted_iota(jnp.int32, (LANES, LANES), 1)
    lane_r = lax.broadcasted_iota(jnp.int32, (rows, LANES), 1)

    def new_rows(b):
        km_ref[...] = jnp.zeros(km_ref.shape, F32)
        mx_ref[...] = jnp.full(mx_ref.shape, -jnp.inf, F32)
        lx_ref[...] = jnp.zeros(lx_ref.shape, F32)

    def past_chunk(b, c, slot):
        q = q_ref[b]
        k_refs, v_refs = _page_refs(kbuf, slot), _page_refs(vbuf, slot)
        last = jnp.where(c == n_chunks - 1, 1, 0)
        km, mx, lx = km_ref[...], mx_ref[...], lx_ref[...]
        for n in range(nbc):
            blk = c * nbc + n
            kt = _cat_t(k_refs[2 * n:2 * n + 2])
            vt = _cat_t(v_refs[2 * n:2 * n + 2])
            km = jnp.where(lane_f == blk, jnp.sum(kt, axis=1, keepdims=True) * (1.0 / TQ), km)
            s = _dot(q, kt.astype(BF16))
            dd = last if n == nbc - 1 else 0
            blocks = [s[r * nq:(r + 1) * nq] + bias_ref[h, dd] for r, h in enumerate(HEAD_PERM)]
            m, l, acc = _softmax_step(blocks, *_init_state(rows), vt.astype(BF16), v_is_t=True)
            mx = jnp.where(lane_r == blk, m, mx)
            lx = jnp.where(lane_r == blk, l, lx)
            acc_ref[blk] = acc
        km_ref[...] = km
        mx_ref[...] = mx
        lx_ref[...] = lx

    def finish(b):
        q = q_ref[b]
        g = _dot(qf_ref[b], km_ref[...], precision=lax.Precision.HIGHEST)
        rank = jnp.zeros(g.shape, F32)
        for mth in range(nb):
            col = g[:, mth:mth + 1]
            beats = (col > g) | ((col == g) & (lane_r > mth))
            rank = rank + jnp.where(beats, 1.0, 0.0)
        sel = (rank < float(MOBA_TOPK)) & (lane_r < nb)
        s = _dot_nt(q, kn_ref[b].astype(BF16))
        blocks = [s[r * nq:(r + 1) * nq] + own_ref[h] for r, h in enumerate(HEAD_PERM)]
        m_o, l_o, acc_o = _softmax_step(blocks, *_init_state(rows), vn_ref[b].astype(BF16))
        mx = mx_ref[...]
        m_all = jnp.maximum(jnp.max(jnp.where(sel, mx, -jnp.inf), axis=-1, keepdims=True), m_o)
        w = jnp.where(sel, jnp.exp(mx - m_all), 0.0)
        w_o = jnp.exp(m_o - m_all)
        den = jnp.sum(w * lx_ref[...], axis=-1, keepdims=True) + w_o * l_o
        num = w_o * acc_o
        for n in range(nb):
            num = num + w[:, n:n + 1] * acc_ref[n]
        o_ref[b] = num / den

    _paged_walk(pt_ref, layer, q_ref.shape[0], n_chunks, False, [(ck_ref, kbuf), (cv_ref, vbuf)], sems,
                new_rows, past_chunk, finish)


def _moba_decode(layer, page_table, q, qf, bias, own, kn, vn, cache_k, cache_v):
    b, rows, _ = q.shape
    P = PAGES_PER_STEP
    n_chunks = page_table.shape[1] // P
    nb = n_chunks * P // 2
    assert nb <= LANES
    buf = pltpu.VMEM((2, P, KV_HEADS, HEAD_DIM, LANES), F32)
    return _decode_call(
        functools.partial(_moba_decode_kernel, layer=layer, n_chunks=n_chunks), "moba_decode", page_table,
        [q, qf, bias, own, kn, vn], [cache_k, cache_v], [buf, buf],
        [pltpu.VMEM((LANES, LANES), F32), pltpu.VMEM((rows, LANES), F32), pltpu.VMEM((rows, LANES), F32),
         pltpu.VMEM((nb, rows, LANES), F32)],
        jax.ShapeDtypeStruct((b, rows, LANES), F32))


def _diff_finish(l, acc, lam, g2, lam_init, t):
    o = acc / l
    o_h0 = o[:t] - lam * o[t:2 * t]
    o_h1 = o[2 * t:3 * t] - lam * o[3 * t:]
    pair = jnp.where(_lane_lo(o_h0.shape), o_h0, o_h1)
    return _subln(pair, g2, lam_init)


def _diff_decode_kernel(pt_ref, q_ref, kn_ref, vn_ref, own_ref, bias_ref, cl_ref, g2_ref, ck_ref, cv_ref, o_ref,
                        kbuf, vbuf, sems, m_ref, l_ref, acc_ref, *, layer, n_chunks, lam_init):
    P = PAGES_PER_STEP
    nq = own_ref.shape[1]
    scale = C_QK_DIM ** -0.5
    heads_of = lambda sl: [2 * sl + r // 2 for r in range(4)]

    def new_rows(b):
        for sl in range(2):
            lanes = slice(sl * LANES, (sl + 1) * LANES)
            s = _dot_nt(q_ref[b, sl], kn_ref[b, :, lanes].astype(BF16)) * scale
            blocks = [s[r * nq:(r + 1) * nq] + own_ref[h] for r, h in enumerate(heads_of(sl))]
            _state_update(m_ref.at[sl], l_ref.at[sl], acc_ref.at[sl], blocks,
                          vn_ref[b, :, lanes].astype(BF16), first=True)

    def past_chunk(b, c, slot):
        k_refs, v_refs = _page_refs(kbuf, slot), _page_refs(vbuf, slot)
        last = jnp.where(c == n_chunks - 1, 1, 0)
        for sl in range(2):
            hs = slice(2 * sl, 2 * sl + 2)
            s = _dot(q_ref[b, sl], _cat_t(k_refs, hs).astype(BF16)) * scale
            blocks = []
            for r, h in enumerate(heads_of(sl)):
                bias = jnp.concatenate([bias_ref[h, 0]] * (P - 1) + [bias_ref[h, last]], axis=1)
                blocks.append(s[r * nq:(r + 1) * nq] + bias)
            _state_update(m_ref.at[sl], l_ref.at[sl], acc_ref.at[sl], blocks,
                          _cat_t(v_refs, hs).astype(BF16), first=False, v_is_t=True)

    def finish(b):
        lam = _diff_lambda(cl_ref[...], lam_init)
        for sl in range(2):
            o_ref[b, :, sl * LANES:(sl + 1) * LANES] = _diff_finish(
                l_ref[sl], acc_ref[sl], lam, g2_ref[...], lam_init, nq).astype(BF16)

    _paged_walk(pt_ref, layer, q_ref.shape[0], n_chunks, False, [(ck_ref, kbuf), (cv_ref, vbuf)], sems,
                new_rows, past_chunk, finish)


def _diff_decode(layer, page_table, q, kn, vn, own, bias, cl, g2, cache_k, cache_v, lam_init):
    b, _, rows, _ = q.shape
    nq = rows // 4
    P = PAGES_PER_STEP
    n_chunks = page_table.shape[1] // P
    buf = pltpu.VMEM((2, P, N_HEADS, HEAD_DIM, LANES), F32)
    return _decode_call(
        functools.partial(_diff_decode_kernel, layer=layer, n_chunks=n_chunks, lam_init=lam_init), "diff_decode",
        page_table, [q, kn, vn, own, bias, cl, g2], [cache_k, cache_v], [buf, buf],
        [pltpu.VMEM((2, rows, 1), F32), pltpu.VMEM((2, rows, 1), F32), pltpu.VMEM((2, rows, LANES), F32)],
        jax.ShapeDtypeStruct((b, nq, 256), BF16))


def _mla_out(o, wuv_ref, t):
    out = _dot(o[:t].astype(BF16), wuv_ref[0])
    for h in range(1, N_HEADS):
        out = out + _dot(o[h * t:(h + 1) * t].astype(BF16), wuv_ref[h])
    return out


def _mla_decode_kernel(pt_ref, ql_ref, qr_ref, cn_ref, rn_ref, own_ref, wuv_ref, cc_ref, cr_ref, o_ref,
                       cbuf, rbuf, sems, m_ref, l_ref, acc_ref, *, layer, n_chunks):
    nq = own_ref.shape[0]
    scale = (D_NOPE + D_ROPE) ** -0.5

    def new_rows(b):
        ckv = cn_ref[b].astype(BF16)
        s = (_dot_nt(ql_ref[b], ckv) + _dot_nt(qr_ref[b], rn_ref[b].astype(BF16))) * scale
        blocks = [s[r * nq:(r + 1) * nq] + own_ref[...] for r in range(N_HEADS)]
        _state_update(m_ref, l_ref, acc_ref, blocks, ckv, first=True)

    def past_chunk(b, c, slot):
        ckv = jnp.concatenate([r[...] for r in _page_refs(cbuf, slot)], axis=0).astype(BF16)
        krt = jnp.concatenate([r[...] for r in _page_refs(rbuf, slot)], axis=1).astype(BF16)
        s = (_dot_nt(ql_ref[b], ckv) + _dot(qr_ref[b], krt)) * scale
        blocks = [s[r * nq:(r + 1) * nq] for r in range(N_HEADS)]
        _state_update(m_ref, l_ref, acc_ref, blocks, ckv, first=False)

    def finish(b):
        o_ref[b] = _mla_out(acc_ref[...] / l_ref[...], wuv_ref, nq).astype(BF16)

    _paged_walk(pt_ref, layer, ql_ref.shape[0], n_chunks, False, [(cc_ref, cbuf), (cr_ref, rbuf)], sems,
                new_rows, past_chunk, finish)


def _mla_decode(layer, page_table, ql, qr, cn, rn, own, wuv, cache_ckv, cache_kr):
    b, rows, _ = ql.shape
    nq = rows // N_HEADS
    P = PAGES_PER_STEP
    n_chunks = page_table.shape[1] // P
    return _decode_call(
        functools.partial(_mla_decode_kernel, layer=layer, n_chunks=n_chunks), "mla_decode", page_table,
        [ql, qr, cn, rn, own, wuv], [cache_ckv, cache_kr],
        [pltpu.VMEM((2, P, LANES, LANES), F32), pltpu.VMEM((2, P, D_ROPE, LANES), F32)],
        [pltpu.VMEM((rows, 1), F32), pltpu.VMEM((rows, 1), F32), pltpu.VMEM((rows, LANES), F32)],
        jax.ShapeDtypeStruct((b, nq, 256), BF16))


def _bucket_table():
    d = np.arange(MAX_DISTANCE + 1)
    max_exact = N_BUCKETS // 2
    df = np.maximum(d, 1).astype(np.float32)
    large = max_exact + (np.log(df / max_exact) / math.log(MAX_DISTANCE / max_exact)
                         * (N_BUCKETS - max_exact)).astype(np.int32)
    large = np.minimum(large, N_BUCKETS - 1)
    return np.where(d < max_exact, d, large)


def _tables(rel_bias, t, nq, q0, bs):
    bd = rel_bias[_bucket_table()].T
    r = np.arange(TQ)[:, None]
    c = np.arange(TQ)[None, :]
    idx0 = np.clip(r - c, 0, MAX_DISTANCE)
    idx1 = np.clip(TQ + r - c, 0, MAX_DISTANCE)
    far = jnp.broadcast_to(bd[:, MAX_DISTANCE][:, None, None], (bd.shape[0], TQ, TQ))
    prompt = jnp.stack([jnp.where(r >= c, bd[:, idx0], NEG_INF), bd[:, idx1], far], axis=1)
    mask2 = jnp.stack([jnp.where(r >= c, 0.0, NEG_INF).astype(F32), jnp.zeros((TQ, TQ), F32)])
    tq = np.arange(nq)[:, None]
    u = np.arange(LANES)[None, :]
    own_ok = (u <= tq) & (u < nq)
    own = jnp.where(own_ok, bd[:, np.clip(tq - u, 0, MAX_DISTANCE)], NEG_INF)
    own_mask = jnp.where(own_ok, 0.0, NEG_INF).astype(F32)
    idx_last = np.clip(LANES + tq - u, 0, MAX_DISTANCE)
    page_far = jnp.broadcast_to(bd[:, MAX_DISTANCE][:, None, None], (bd.shape[0], nq, LANES))
    page = jnp.stack([page_far, bd[:, idx_last]], axis=1)
    blk = jnp.stack([jnp.concatenate([page_far, page_far], axis=-1),
                     jnp.concatenate([page_far, bd[:, idx_last]], axis=-1)], axis=1)
    cos_p, sin_p = _rope_tables(jnp.arange(t))
    cos_s, sin_s = _rope_tables(q0 + jnp.arange(nq))
    return dict(prompt_t=jnp.swapaxes(prompt, -1, -2), mask_t=jnp.swapaxes(mask2, -1, -2),
                own=own, own_mask=own_mask, page=page, blk=blk,
                cos_p=cos_p, sin_p=sin_p, cos_pt=cos_p.T, sin_pt=sin_p.T,
                cos_s=jnp.tile(cos_s, (bs, 1)), sin_s=jnp.tile(sin_s, (bs, 1)))


def _rope_tables(pos):
    half = D_ROPE // 2
    inv = jnp.power(ROPE_BASE, -jnp.arange(half, dtype=F32) / half)
    ang = pos.astype(F32)[:, None] * inv
    c, s = jnp.cos(ang), jnp.sin(ang)
    pad = jnp.zeros((pos.shape[0], LANES - D_ROPE), F32)
    return jnp.concatenate([c, c, pad], axis=1), jnp.concatenate([-s, s, pad], axis=1)


def _perm_heads(w, axis):
    parts = jnp.split(w, N_HEADS, axis=axis)
    return jnp.concatenate([parts[h] for h in HEAD_PERM], axis=axis)


def _layer_weights(l, norm_g, w_in, b_forget, b_gate, d_q_norm_g, d_w_q_up, d_kv_norm_g, d_w_kv_up,
                   c_subln_g, w_branch, w_out):
    w = w_in[l]
    d = w.shape[0]
    splits = (256, 128, 128, 4, 256, 256, 128, 128, 256, 256, 256, 256, 256, 256, 128, 32, 256, 4 * d)
    offs = np.cumsum((0,) + splits)
    (a_q, a_k, a_v, a_f, a_z, b_q, b_k, b_v, b_z, c_q, c_k, c_v, c_z, d_qa, d_kva, d_kr, d_z, gates) = [
        w[:, offs[i]:offs[i + 1]] for i in range(len(splits))]
    scale = HEAD_DIM ** -0.5
    zpad = lambda x, n: jnp.concatenate([x, jnp.zeros((d, n - x.shape[1]), x.dtype)], axis=1)
    swap = jnp.concatenate([d_kr[:, D_ROPE // 2:], d_kr[:, :D_ROPE // 2]], axis=1)
    segs = dict(qa=_perm_heads(a_q, 1) * scale, ka=a_k, va=a_v, qb=_perm_heads(b_q, 1) * scale, kb=b_k, vb=b_v,
                qc=c_q, kc=c_k, vc=c_v, dqa=d_qa, dkva=d_kva, kr=zpad(d_kr, LANES), krs=zpad(swap, LANES),
                z=jnp.concatenate([_perm_heads(a_z, 1), _perm_heads(b_z, 1), c_z, d_z], axis=1))
    w_pack = jnp.concatenate([zpad(a_f, LANES) if n == "af" else segs[n] for n, _ in _SEGS], axis=1).astype(BF16)
    w_rows = jnp.concatenate([segs[n] for n, _ in _SEGS_R], axis=1).astype(BF16)
    w_t = jnp.concatenate([zpad(a_f, 8) if n == "af" else segs[n] for n, _ in _SEGS_T], axis=1).astype(BF16).T
    wq = d_w_q_up[l]
    r = wq.shape[0]
    nope = wq[:, :, :D_NOPE].reshape(r, N_HEADS * D_NOPE)
    rope = wq[:, :, D_NOPE:]
    rope_sw = jnp.concatenate([rope[..., D_ROPE // 2:], rope[..., :D_ROPE // 2]], axis=-1)

    def spread(x):
        z1 = jnp.zeros((r, N_HEADS, D_LAT), x.dtype)
        z2 = jnp.zeros((r, N_HEADS, 256 - D_LAT - D_ROPE), x.dtype)
        return jnp.concatenate([z1, x, z2], axis=-1).reshape(r, N_HEADS * 256)

    wq_pack = jnp.concatenate([nope, spread(rope), spread(rope_sw)], axis=1).astype(BF16)
    wkv = d_w_kv_up[l]
    w_uk = wkv[:, :, :D_NOPE]
    w_uv = wkv[:, :, D_NOPE:]
    wuk = jnp.zeros((N_HEADS, D_NOPE, N_HEADS, 256), F32)
    wuv = jnp.zeros((N_HEADS, D_LAT, N_HEADS, D_V), F32)
    for h in range(N_HEADS):
        wuk = wuk.at[h, :, h, :D_LAT].set(w_uk[:, h, :].T)
        wuv = wuv.at[h, :, h, :].set(w_uv[:, h, :])
    wuk = wuk.reshape(N_HEADS * D_NOPE, N_HEADS * 256).astype(BF16)
    wuv = wuv.reshape(N_HEADS, D_LAT, N_HEADS * D_V).astype(BF16)
    wbr = w_branch[l]
    wbr = jnp.stack([_perm_heads(wbr[0], 0), _perm_heads(wbr[1], 0), wbr[2], wbr[3]])
    bf = jnp.concatenate([b_forget[l].astype(F32), jnp.zeros((LANES - N_HEADS,), F32)])
    g2 = jnp.concatenate([c_subln_g[l], c_subln_g[l]])
    return dict(
        norm_g=norm_g[l][None, :], w_pack=w_pack, w_rows=w_rows, w_t=w_t,
        b_forget=bf[None, :], bf_col=bf[:8, None],
        gq=d_q_norm_g[l][None, :], gq_col=d_q_norm_g[l][:, None],
        gkv=d_kv_norm_g[l][None, :], gkv_col=d_kv_norm_g[l][:, None],
        wq=wq_pack, wq_t=wq_pack.T, wuk=wuk, wuk_t=wuk.T, wuv=wuv, wuv_t=jnp.swapaxes(wuv, 1, 2),
        wg=gates.astype(BF16), b_gate=b_gate[l][None, :], wbr=wbr.astype(BF16), wout=w_out[l].astype(BF16),
        g2=g2[None, :], g2_col=g2[:, None])


def _unbd(o, nq):
    lo = jnp.arange(LANES) < LANES // 2
    left = jnp.where(lo, o[:, :nq], o[:, nq:2 * nq])
    right = jnp.where(lo, o[:, 2 * nq:3 * nq], o[:, 3 * nq:])
    return jnp.concatenate([left, right], axis=-1).reshape(-1, 256)


def _pad_page(x):
    return jnp.pad(x, ((0, 0), (0, LANES - x.shape[1]), (0, 0)))


def kernel(x_prompt, x_sample, cache_a_k, cache_a_v, cache_a_logf, cache_b_k, cache_b_v, cache_c_k, cache_c_v, cache_d_ckv, cache_d_kr, page_table, norm_g, w_in, b_forget, b_gate, d_q_norm_g, d_w_q_up, d_kv_norm_g, d_w_kv_up, c_lambda, c_subln_g, w_branch, w_out, rel_bias, final_norm_g):
    bp, t, d = x_prompt.shape
    bs, nq, _ = x_sample.shape
    depth, n_phys, page = cache_a_k.shape[:3]
    n_pages = page_table.shape[1]
    q0 = n_pages * page
    assert page == LANES and t % TQ == 0 and n_pages % PAGES_PER_STEP == 0 and nq % 8 == 0

    t5 = lambda c: jnp.transpose(c, (0, 1, 3, 4, 2))
    ca_k, ca_v, cb_k, cb_v, cc_k, cc_v = map(t5, (cache_a_k, cache_a_v, cache_b_k, cache_b_v, cache_c_k, cache_c_v))
    ca_lf = jnp.swapaxes(cache_a_logf, 2, 3)
    cd_kr = jnp.swapaxes(cache_d_kr, 2, 3)

    tabs = _tables(rel_bias.astype(F32), t, nq, q0, bs)
    tm_s = bs * nq
    final_g = final_norm_g[None, :]
    rows = 4 * nq
    nbc = PAGES_PER_STEP // 2

    hp = x_prompt.reshape(bp * t, d)
    hs = x_sample.reshape(bs * nq, d)
    rows_p, rows_s = [], []
    for l in range(depth):
        lam_init = 0.8 - 0.6 * math.exp(-0.3 * l)
        lw = _layer_weights(l, norm_g, w_in, b_forget, b_gate, d_q_norm_g, d_w_q_up, d_kv_norm_g, d_w_kv_up,
                            c_subln_g, w_branch, w_out)
        cl = c_lambda[l].astype(F32)
        last = l == depth - 1

        pr = _inproj_prompt(hp, lw, tabs, bp, t, 256)
        r3 = lambda a: a.reshape(bp, t, a.shape[-1])
        frow = _cumsum_rows(pr["lfT"])
        fkrep = jnp.broadcast_to(frow[:, :N_HEADS, :, None], (bp, N_HEADS, t, LANES))
        o_a = _fox_prompt(pr["qaT"], r3(pr["ka"]), pr["vaT"], frow, fkrep, tabs["mask_t"])
        o_b = _moba_prompt(pr["qbT"], pr["qbfT"], r3(pr["kb"]), pr["vbT"], tabs["prompt_t"][:N_HEADS])
        o_c = _diff_prompt(pr["qcT"], r3(pr["kc"]), pr["vcT"], tabs["prompt_t"][N_HEADS:], cl, lw["g2_col"], lam_init)
        o_d = _mla_prompt(pr["qdT"], r3(pr["ckv"]), r3(pr["kr"]), pr["ckvT"], tabs["mask_t"], lw["wuv_t"])
        f2 = lambda a: a.reshape(bp * t, a.shape[-1])
        hp = _merge(hp, f2(o_a), f2(o_b), f2(o_c), f2(o_d), pr["z"], lw, final_g, last, 512)
        heads_t = lambda a, nh: jnp.transpose(a.reshape(bp, nh, -1, t), (0, 3, 1, 2))
        rows_p.append((heads_t(pr["kaT"], KV_HEADS), heads_t(pr["vaT"], KV_HEADS),
                       jnp.swapaxes(pr["lfT"][:, :N_HEADS], 1, 2),
                       heads_t(pr["kbT"], KV_HEADS), heads_t(pr["vbT"], KV_HEADS),
                       heads_t(pr["kcT"], N_HEADS), heads_t(pr["vcT"], N_HEADS),
                       r3(pr["ckv"]), jnp.swapaxes(pr["krT"][:, :D_ROPE], 1, 2)))

        sr = _inproj(hs, lw, tabs["cos_s"], tabs["sin_s"], tm_s)
        s3 = lambda a: a.reshape(bs, nq, a.shape[-1])
        lf_s = s3(sr["lf"])[:, :, :N_HEADS]
        qa = s3(sr["qa"])
        q_bd = jnp.concatenate([_blockdiag_rows_host(qa[..., :LANES], 2), _blockdiag_rows_host(qa[..., LANES:], 2)], axis=1)
        lfn = jnp.pad(jnp.swapaxes(lf_s, 1, 2), ((0, 0), (0, 8 - N_HEADS), (0, LANES - nq)))
        o = _fox_decode(l, page_table, q_bd, _pad_page(s3(sr["ka"])), _pad_page(s3(sr["va"])), lfn,
                        tabs["own_mask"], ca_k, ca_v, ca_lf)
        o_a = _unbd(o, nq).astype(BF16)
        qb = s3(sr["qb"])
        qbf = s3(sr["qbf"])
        q_bd = jnp.concatenate([_blockdiag_rows_host(qb[..., :LANES], 2), _blockdiag_rows_host(qb[..., LANES:], 2)], axis=1)
        qf_bd = jnp.concatenate([_blockdiag_rows_host(qbf[..., :LANES], 2), _blockdiag_rows_host(qbf[..., LANES:], 2)], axis=1)
        o = _moba_decode(l, page_table, q_bd, qf_bd, tabs["blk"][:N_HEADS], tabs["own"][:N_HEADS],
                         _pad_page(s3(sr["kb"])), _pad_page(s3(sr["vb"])), cb_k, cb_v)
        o_b = _unbd(o, nq).astype(BF16)
        qc = s3(sr["qc"])
        q4 = jnp.stack([_blockdiag_rows_host(qc[..., :LANES], 4), _blockdiag_rows_host(qc[..., LANES:], 4)], axis=1)
        o_c = _diff_decode(l, page_table, q4, _pad_page(s3(sr["kc"])), _pad_page(s3(sr["vc"])),
                           tabs["own"][N_HEADS:], tabs["page"][N_HEADS:], cl, lw["g2"], cc_k, cc_v, lam_init)
        o_c = o_c.reshape(bs * nq, 256)
        qd = jnp.swapaxes(s3(sr["qd"]).reshape(bs, nq, N_HEADS, 256), 1, 2).reshape(bs, rows, 256)
        o_d = _mla_decode(l, page_table, qd[..., :D_LAT], qd[..., D_LAT:D_LAT + D_ROPE],
                          _pad_page(s3(sr["ckv"])), _pad_page(s3(sr["kr"])[..., :D_ROPE]),
                          tabs["own_mask"], lw["wuv"], cache_d_ckv, cd_kr)
        o_d = o_d.reshape(bs * nq, 256)
        hs = _merge(hs, o_a, o_b, o_c, o_d, sr["z"], lw, final_g, last, tm_s)
        rows_s.append((sr["ka"].reshape(bs, nq, KV_HEADS, HEAD_DIM), sr["va"].reshape(bs, nq, KV_HEADS, HEAD_DIM), lf_s,
                       sr["kb"].reshape(bs, nq, KV_HEADS, HEAD_DIM), sr["vb"].reshape(bs, nq, KV_HEADS, HEAD_DIM),
                       sr["kc"].reshape(bs, nq, N_HEADS, 2 * C_QK_DIM), sr["vc"].reshape(bs, nq, N_HEADS, HEAD_DIM),
                       s3(sr["ckv"]), s3(sr["kr"])[:, :, :D_ROPE]))

    stack = lambda rows, i: jnp.stack([r[i] for r in rows], axis=0)
    return ((hp.reshape(bp, t, d), hs.reshape(bs, nq, d))
            + tuple(stack(rows_p, i) for i in range(9)) + tuple(stack(rows_s, i) for i in range(9)))
```

```python
import functools
import math

import jax
import jax.numpy as jnp
import numpy as np
from jax import lax
from jax.experimental import pallas as pl
from jax.experimental.pallas import tpu as pltpu

F32 = jnp.float32
BF16 = jnp.bfloat16

HEAD_DIM = 64
N_HEADS = 4
KV_HEADS = 2
MOBA_TOPK = 3
C_QK_DIM = 32
D_NOPE = 64
D_ROPE = 32
D_V = 64
D_LAT = 128
ROPE_BASE = 10000.0
N_BUCKETS = 32
MAX_DISTANCE = 128
EPS = 1e-6
NEG_INF = -1e30

LANES = 128
TQ = 256
PAGES_PER_STEP = 32
VMEM_LIMIT = 56 * 1024 * 1024

HEAD_PERM = (0, 2, 1, 3)

_SEGS = (("qa", 256), ("ka", 128), ("va", 128), ("qb", 256), ("kb", 128), ("vb", 128),
         ("qc", 256), ("kc", 256), ("vc", 256), ("dqa", 256), ("dkva", 128),
         ("kr", 128), ("krs", 128), ("af", 128), ("z", 1024))
_SEGS_R = (("ka", 128), ("kb", 128), ("kc", 256), ("dkva", 128), ("kr", 128), ("krs", 128), ("z", 1024))
_SEGS_T = (("qa", 256), ("qb", 256), ("qc", 256), ("ka", 128), ("va", 128), ("kb", 128), ("vb", 128),
           ("kc", 256), ("vc", 256), ("dqa", 256), ("dkva", 128), ("kr", 128), ("krs", 128), ("af", 8))


def _offsets(segs):
    off, o = {}, 0
    for n, w in segs:
        off[n] = (o, o + w)
        o += w
    return off


_OFF = _offsets(_SEGS)
_OFF_R = _offsets(_SEGS_R)
_OFF_T = _offsets(_SEGS_T)


def _cparams(sem):
    return pltpu.CompilerParams(dimension_semantics=sem, vmem_limit_bytes=VMEM_LIMIT)


def _dot(a, b, precision=None):
    return jnp.dot(a, b, preferred_element_type=F32, precision=precision)


def _dot_nt(a, b, precision=None):
    return lax.dot_general(a, b, (((1,), (1,)), ((), ())), preferred_element_type=F32,
                           precision=precision)


def _rms(x, g):
    return x * lax.rsqrt(jnp.mean(x * x, axis=-1, keepdims=True) + EPS) * g


def _rms_t(x, g_col):
    return x * lax.rsqrt(jnp.mean(x * x, axis=0, keepdims=True) + EPS) * g_col


def _log_sigmoid(x):
    return jnp.minimum(x, 0.0) - jnp.log(1.0 + jnp.exp(-jnp.abs(x)))


def _inproj_kernel(x_ref, g_ref, w_ref, bf_ref, gq_ref, gkv_ref, wq_ref, wuk_ref, cos_ref, sin_ref,
                   qa_ref, ka_ref, va_ref, lf_ref, qb_ref, qbf_ref, kb_ref, vb_ref,
                   qc_ref, kc_ref, vc_ref, qd_ref, ckv_ref, kr_ref, z_ref):
    hb = _rms(x_ref[...], g_ref[...]).astype(BF16)
    proj = _dot(hb, w_ref[...])

    def seg(name):
        lo, hi = _OFF[name]
        return proj[:, lo:hi]

    qa_ref[...] = seg("qa").astype(BF16)
    ka_ref[...] = seg("ka")
    va_ref[...] = seg("va")
    lf_ref[...] = _log_sigmoid(seg("af") + bf_ref[...])
    qb = seg("qb")
    qb_ref[...] = qb.astype(BF16)
    qbf_ref[...] = qb
    kb_ref[...] = seg("kb")
    vb_ref[...] = seg("vb")
    qc_ref[...] = seg("qc").astype(BF16)
    kc_ref[...] = seg("kc")
    vc_ref[...] = seg("vc")
    z_ref[...] = seg("z").astype(BF16)
    ckv_ref[...] = _rms(seg("dkva"), gkv_ref[...])
    cos = cos_ref[...]
    sin = sin_ref[...]
    kr_ref[...] = seg("kr") * cos + seg("krs") * sin
    qn = _rms(seg("dqa"), gq_ref[...]).astype(BF16)
    qq = _dot(qn, wq_ref[...])
    q_lat = _dot(qq[:, :256].astype(BF16), wuk_ref[...])
    cos8 = jnp.concatenate([cos] * 8, axis=1)
    sin8 = jnp.concatenate([sin] * 8, axis=1)
    q_rope = qq[:, 256:1280] * cos8 + qq[:, 1280:2304] * sin8
    qd_ref[...] = (q_lat + q_rope).astype(BF16)


def _inproj(x, lw, cos_t, sin_t, tm):
    m = x.shape[0]
    row = lambda w: pl.BlockSpec((tm, w), lambda i: (i, 0))
    full = lambda a: pl.BlockSpec(a.shape, lambda i: (0,) * a.ndim)
    outs = (("qa", 256, BF16), ("ka", 128, F32), ("va", 128, F32), ("lf", 128, F32),
            ("qb", 256, BF16), ("qbf", 256, F32), ("kb", 128, F32), ("vb", 128, F32),
            ("qc", 256, BF16), ("kc", 256, F32), ("vc", 256, F32), ("qd", 1024, BF16),
            ("ckv", 128, F32), ("kr", 128, F32), ("z", 1024, BF16))
    ins = (x, lw["norm_g"], lw["w_pack"], lw["b_forget"], lw["gq"], lw["gkv"], lw["wq"], lw["wuk"], cos_t, sin_t)
    res = pl.pallas_call(
        _inproj_kernel,
        grid=(m // tm,),
        in_specs=[row(x.shape[1])] + [full(a) for a in ins[1:8]] + [row(LANES), row(LANES)],
        out_specs=[row(w) for _, w, _ in outs],
        out_shape=[jax.ShapeDtypeStruct((m, w), dt) for _, w, dt in outs],
        compiler_params=_cparams(("arbitrary",)),
        name="inproj",
    )(*ins)
    return {n: r for (n, _, _), r in zip(outs, res)}


_P_ROW_OUTS = (("ka", 128, BF16), ("kb", 128, F32), ("kc", 256, BF16), ("ckv", 128, F32), ("kr", 128, F32),
               ("z", 1024, BF16))
_P_T_OUTS = (("qaT", 256, BF16), ("qbT", 256, BF16), ("qbfT", 256, F32), ("qcT", 256, BF16), ("qdT", 1024, BF16),
             ("kaT", 128, F32), ("vaT", 128, F32), ("kbT", 128, F32), ("vbT", 128, F32), ("kcT", 256, F32),
             ("vcT", 256, F32), ("ckvT", 128, BF16), ("krT", 128, F32), ("lfT", 8, F32))


def _inproj_prompt_kernel(x_ref, g_ref, wr_ref, wt_ref, bfc_ref, gqc_ref, gkv_ref, gkvc_ref, wqt_ref, wukt_ref,
                          cos_ref, sin_ref, cost_ref, sint_ref, *outs):
    o = {n: r for (n, _, _), r in zip(_P_ROW_OUTS + _P_T_OUTS, outs)}
    hb = _rms(x_ref[...], g_ref[...]).astype(BF16)
    pr = _dot(hb, wr_ref[...])

    def seg(name):
        lo, hi = _OFF_R[name]
        return pr[:, lo:hi]

    o["ka"][...] = seg("ka").astype(BF16)
    o["kb"][...] = seg("kb")
    o["kc"][...] = seg("kc").astype(BF16)
    o["ckv"][...] = _rms(seg("dkva"), gkv_ref[...])
    o["kr"][...] = seg("kr") * cos_ref[...] + seg("krs") * sin_ref[...]
    o["z"][...] = seg("z").astype(BF16)

    pt = _dot_nt(wt_ref[...], hb)

    def segt(name):
        lo, hi = _OFF_T[name]
        return pt[lo:hi]

    o["qaT"][0] = segt("qa").astype(BF16)
    qb = segt("qb")
    o["qbT"][0] = qb.astype(BF16)
    o["qbfT"][0] = qb
    o["qcT"][0] = segt("qc").astype(BF16)
    for n in ("ka", "va", "kb", "vb", "kc", "vc"):
        o[n + "T"][0] = segt(n)
    o["lfT"][0] = _log_sigmoid(segt("af") + bfc_ref[...])
    cost = cost_ref[...]
    sint = sint_ref[...]
    o["krT"][0] = segt("kr") * cost + segt("krs") * sint
    o["ckvT"][0] = _rms_t(segt("dkva"), gkvc_ref[...]).astype(BF16)
    qn = _rms_t(segt("dqa"), gqc_ref[...]).astype(BF16)
    qq = _dot(wqt_ref[...], qn)
    q_lat = _dot(wukt_ref[...], qq[:256].astype(BF16))
    cos8 = jnp.concatenate([cost] * 8, axis=0)
    sin8 = jnp.concatenate([sint] * 8, axis=0)
    o["qdT"][0] = (q_lat + qq[256:1280] * cos8 + qq[1280:2304] * sin8).astype(BF16)


def _inproj_prompt(x, lw, tabs, b, t, tm):
    m = x.shape[0]
    n_t = t // tm
    row = lambda w: pl.BlockSpec((tm, w), lambda i: (i, 0))
    full = lambda a: pl.BlockSpec(a.shape, lambda i: (0,) * a.ndim)
    tr = lambda w: pl.BlockSpec((1, w, tm), lambda i: (i // n_t, 0, i % n_t))
    ins = (x, lw["norm_g"], lw["w_rows"], lw["w_t"], lw["bf_col"], lw["gq_col"], lw["gkv"], lw["gkv_col"],
           lw["wq_t"], lw["wuk_t"], tabs["cos_p"], tabs["sin_p"], tabs["cos_pt"], tabs["sin_pt"])
    res = pl.pallas_call(
        _inproj_prompt_kernel,
        grid=(m // tm,),
        in_specs=[row(x.shape[1])] + [full(a) for a in ins[1:10]]
        + [pl.BlockSpec((tm, LANES), lambda i: (i % n_t, 0))] * 2
        + [pl.BlockSpec((LANES, tm), lambda i: (0, i % n_t))] * 2,
        out_specs=[row(w) for _, w, _ in _P_ROW_OUTS] + [tr(w) for _, w, _ in _P_T_OUTS],
        out_shape=[jax.ShapeDtypeStruct((m, w), dt) for _, w, dt in _P_ROW_OUTS]
        + [jax.ShapeDtypeStruct((b, w, t), dt) for _, w, dt in _P_T_OUTS],
        compiler_params=_cparams(("arbitrary",)),
        name="inproj_prompt",
    )(*ins)
    return {n: r for (n, _, _), r in zip(_P_ROW_OUTS + _P_T_OUTS, res)}


def _merge_kernel(x_ref, oa_ref, ob_ref, oc_ref, od_ref, z_ref, g_ref, wg_ref, bg_ref, wbr_ref,
                  wout_ref, fg_ref, y_ref, *, final):
    x = x_ref[...]
    hb = _rms(x, g_ref[...]).astype(BF16)
    d = x.shape[1]
    acc = jnp.zeros(x.shape, F32)
    for n, o_ref in enumerate((oa_ref, ob_ref, oc_ref, od_ref)):
        w = o_ref.shape[1]
        z = z_ref[:, n * w:(n + 1) * w].astype(F32)
        a = (o_ref[...].astype(F32) * (z * jax.nn.sigmoid(z))).astype(BF16)
        u = _dot(a, wbr_ref[n])
        gate = _dot(hb, wg_ref[:, n * d:(n + 1) * d]) + bg_ref[:, n * d:(n + 1) * d]
        acc = acc + jax.nn.sigmoid(gate) * u
    y = x + _dot(acc.astype(BF16), wout_ref[...])
    if final:
        y = _rms(y, fg_ref[...])
    y_ref[...] = y


def _merge(x, o_a, o_b, o_c, o_d, z, lw, final_g, final, tm):
    m, d = x.shape
    row = lambda w: pl.BlockSpec((tm, w), lambda i: (i, 0))
    full = lambda a: pl.BlockSpec(a.shape, lambda i: (0,) * a.ndim)
    return pl.pallas_call(
        functools.partial(_merge_kernel, final=final),
        grid=(m // tm,),
        in_specs=[row(d), row(256), row(256), row(256), row(256), row(1024), full(lw["norm_g"]),
                  full(lw["wg"]), full(lw["b_gate"]), full(lw["wbr"]), full(lw["wout"]),
                  full(final_g)],
        out_specs=row(d),
        out_shape=jax.ShapeDtypeStruct((m, d), F32),
        compiler_params=_cparams(("arbitrary",)),
        name="merge",
    )(x, o_a, o_b, o_c, o_d, z, lw["norm_g"], lw["wg"], lw["b_gate"], lw["wbr"], lw["wout"],
      final_g)


def _lane_lo(shape):
    return lax.broadcasted_iota(jnp.int32, shape, len(shape) - 1) < (LANES // 2)


def _softmax_step(s_blocks, m, l, acc, v, v_is_t=False):
    t = s_blocks[0].shape[0]
    p_blocks, m_new_blocks, alpha_blocks, l_blocks = [], [], [], []
    for r, s in enumerate(s_blocks):
        m_old = m[r * t:(r + 1) * t]
        m_new = jnp.maximum(m_old, jnp.max(s, axis=-1, keepdims=True))
        p = jnp.exp(s - m_new)
        alpha = jnp.exp(m_old - m_new)
        l_blocks.append(alpha * l[r * t:(r + 1) * t] + jnp.sum(p, axis=-1, keepdims=True))
        p_blocks.append(p.astype(BF16))
        m_new_blocks.append(m_new)
        alpha_blocks.append(alpha)
    p_all = jnp.concatenate(p_blocks, axis=0)
    alpha_all = jnp.concatenate(alpha_blocks, axis=0)
    pv = _dot_nt(p_all, v) if v_is_t else _dot(p_all, v)
    acc = alpha_all * acc + pv
    return jnp.concatenate(m_new_blocks, axis=0), jnp.concatenate(l_blocks, axis=0), acc


def _init_state(rows):
    return (jnp.full((rows, 1), -jnp.inf, F32), jnp.zeros((rows, 1), F32), jnp.zeros((rows, LANES), F32))


def _subln(pair, g2, lam_init):
    lo = _lane_lo(pair.shape)
    sq = pair * pair
    ss_lo = jnp.sum(jnp.where(lo, sq, 0.0), axis=-1, keepdims=True)
    ss_hi = jnp.sum(jnp.where(lo, 0.0, sq), axis=-1, keepdims=True)
    ms = jnp.where(lo, ss_lo, ss_hi) * (1.0 / D_V)
    return (pair * lax.rsqrt(ms + EPS) * g2) * (1.0 - lam_init)


def _diff_lambda(cl, lam_init):
    a = jnp.sum(cl[0:1] * cl[1:2], axis=-1, keepdims=True)
    b = jnp.sum(cl[2:3] * cl[3:4], axis=-1, keepdims=True)
    return jnp.exp(a) - jnp.exp(b) + lam_init


def _sub_mask(x, lo, hi):
    sub = lax.broadcasted_iota(jnp.int32, x.shape, 0)
    return jnp.where((sub >= lo) & (sub < hi), x, jnp.zeros_like(x))


def _blockdiag_cols(x, n_groups):
    gw = x.shape[0] // n_groups
    return jnp.concatenate([_sub_mask(x, g * gw, (g + 1) * gw) for g in range(n_groups)], axis=1)


def _flash_t(i, qt, k_tile, v_tile, bias_chunk, st_ref, pt_ref, acc_ref, scale=None):
    r_all = qt.shape[1]
    acc_ref[...] = jnp.zeros(acc_ref.shape, F32)

    def body(j, carry, diag):
        m, l = carry
        off = pl.multiple_of(j * TQ, TQ)
        st_ref[...] = _dot(k_tile(off), qt)
        ms, ls, alphas = [], [], []
        for c in range(r_all // LANES):
            cs = slice(c * LANES, (c + 1) * LANES)
            s = st_ref[:, cs]
            if scale is not None:
                s = s * scale
            s = bias_chunk(j, c, s, diag)
            m_prev = m[:, cs]
            m_new = jnp.maximum(m_prev, jnp.max(s, axis=0, keepdims=True))
            p = jnp.exp(s - m_new)
            alpha = jnp.exp(m_prev - m_new)
            ls.append(alpha * l[:, cs] + jnp.sum(p, axis=0, keepdims=True))
            pt_ref[:, cs] = p.astype(BF16)
            ms.append(m_new)
            alphas.append(alpha)
        acc_ref[...] = acc_ref[...] * jnp.concatenate(alphas, axis=1) + _dot(v_tile(off), pt_ref[...])
        return jnp.concatenate(ms, axis=1), jnp.concatenate(ls, axis=1)

    init = (jnp.full((1, r_all), -jnp.inf, F32), jnp.zeros((1, r_all), F32))
    carry = lax.fori_loop(0, i, functools.partial(body, diag=False), init)
    _, l = body(i, carry, True)
    return l


def _eye_bf16(n):
    return jnp.where(lax.broadcasted_iota(jnp.int32, (n, n), 0) == lax.broadcasted_iota(jnp.int32, (n, n), 1),
                     1.0, 0.0).astype(BF16)


def _to_rows(x_t):
    return _dot_nt(_eye_bf16(x_t.shape[1]), x_t.astype(BF16)).astype(BF16)


def _pair_halves(o, s):
    sub = lax.broadcasted_iota(jnp.int32, (LANES, TQ), 0)
    return jnp.where(sub < LANES // 2, o[:, 2 * s * TQ:(2 * s + 1) * TQ], o[:, (2 * s + 1) * TQ:(2 * s + 2) * TQ])


_T_SCRATCH = lambda: [pltpu.VMEM((TQ, 4 * TQ), F32), pltpu.VMEM((TQ, 4 * TQ), BF16), pltpu.VMEM((LANES, 4 * TQ), F32)]


def _cumsum_kernel(x_ref, f_ref):
    x = x_ref[0]
    t = x.shape[1]
    lane = lax.broadcasted_iota(jnp.int32, x.shape, 1) % LANES
    s = 1
    while s < LANES:
        x = x + jnp.where(lane >= s, pltpu.roll(x, s, 1), 0.0)
        s *= 2
    carry = jnp.zeros((x.shape[0], 1), F32)
    for c in range(t // LANES):
        blk = x[:, c * LANES:(c + 1) * LANES] + carry
        f_ref[0, :, c * LANES:(c + 1) * LANES] = blk
        carry = blk[:, LANES - 1:LANES]


def _cumsum_rows(lf_t):
    b, h, t = lf_t.shape
    spec = pl.BlockSpec((1, h, t), lambda i: (i, 0, 0))
    return pl.pallas_call(
        _cumsum_kernel, grid=(b,), in_specs=[spec], out_specs=spec,
        out_shape=jax.ShapeDtypeStruct(lf_t.shape, F32),
        compiler_params=_cparams(("arbitrary",)), name="fox_cumsum",
    )(lf_t)


def _fox_prompt_kernel(q_ref, k_ref, v_ref, frow_ref, fkrep_ref, mask_ref, o_ref, vt_ref, st_ref, pt_ref, acc_ref):
    i = pl.program_id(1)

    @pl.when(i == 0)
    def _():
        vt_ref[...] = v_ref[0].astype(BF16)

    qt = jnp.concatenate([_blockdiag_cols(q_ref[0, :LANES, :], 2), _blockdiag_cols(q_ref[0, LANES:, :], 2)], axis=1)
    q_off = pl.multiple_of(i * TQ, TQ)

    def bias_chunk(j, c, s, diag):
        h = HEAD_PERM[c // 2]
        cc = c % 2
        off = pl.multiple_of(j * TQ, TQ)
        fq = frow_ref[0, h:h + 1, pl.ds(q_off + cc * LANES, LANES)]
        fk = fkrep_ref[0, h, pl.ds(off, TQ), :]
        s = s + (fq - fk)
        return s + mask_ref[0, :, cc * LANES:(cc + 1) * LANES] if diag else s

    l = _flash_t(i, qt, lambda off: k_ref[0, pl.ds(off, TQ), :], lambda off: vt_ref[:, pl.ds(off, TQ)],
                 bias_chunk, st_ref, pt_ref, acc_ref)
    o = acc_ref[...] * (1.0 / l)
    for s in range(2):
        o_ref[0, :, s * LANES:(s + 1) * LANES] = _to_rows(_pair_halves(o, s))


def _fox_prompt(q_t, k, v_t, frow, fkrep, mask_t):
    b, _, t = q_t.shape
    return pl.pallas_call(
        _fox_prompt_kernel,
        grid=(b, t // TQ),
        in_specs=[pl.BlockSpec((1, 256, TQ), lambda bi, i: (bi, 0, i)),
                  pl.BlockSpec((1, t, LANES), lambda bi, i: (bi, 0, 0)),
                  pl.BlockSpec((1, LANES, t), lambda bi, i: (bi, 0, 0)),
                  pl.BlockSpec((1, 8, t), lambda bi, i: (bi, 0, 0)),
                  pl.BlockSpec((1, N_HEADS, t, LANES), lambda bi, i: (bi, 0, 0, 0)),
                  pl.BlockSpec(mask_t.shape, lambda bi, i: (0, 0, 0))],
        out_specs=pl.BlockSpec((1, TQ, 256), lambda bi, i: (bi, i, 0)),
        out_shape=jax.ShapeDtypeStruct((b, t, 256), BF16),
        scratch_shapes=[pltpu.VMEM((LANES, t), BF16)] + _T_SCRATCH(),
        compiler_params=_cparams(("arbitrary", "arbitrary")),
        name="fox_prompt",
    )(q_t, k, v_t, frow, fkrep, mask_t)


def _moba_prompt_kernel(q_ref, qf_ref, k_ref, v_ref, bias_ref, o_ref, kb_ref, vt_ref, km_ref, code_ref,
                        st_ref, pt_ref, acc_ref):
    i = pl.program_id(1)
    t = k_ref.shape[1]
    nb = t // TQ

    @pl.when(i == 0)
    def _():
        kb_ref[...] = k_ref[0].astype(BF16)
        vt_ref[...] = v_ref[0].astype(BF16)
        km_ref[...] = jnp.zeros(km_ref.shape, F32)
        for n in range(nb):
            km_ref[n:n + 1, :] = jnp.mean(k_ref[0, n * TQ:(n + 1) * TQ, :], axis=0, keepdims=True)

    qt = jnp.concatenate([_blockdiag_cols(q_ref[0, :LANES, :], 2), _blockdiag_cols(q_ref[0, LANES:, :], 2)], axis=1)
    qft = jnp.concatenate([_blockdiag_cols(qf_ref[0, :LANES, :], 2), _blockdiag_cols(qf_ref[0, LANES:, :], 2)], axis=1)
    gate = _dot(km_ref[...], qft, precision=lax.Precision.HIGHEST)
    sub = lax.broadcasted_iota(jnp.int32, gate.shape, 0)
    gm = jnp.where(sub < i, gate, NEG_INF)
    rank = jnp.zeros(gate.shape, F32)
    for mth in range(nb):
        row = gm[mth:mth + 1, :]
        beats = (row > gm) | ((row == gm) & (sub > mth))
        rank = rank + jnp.where(beats, 1.0, 0.0)
    keep = ((rank < float(MOBA_TOPK)) & (sub < i)) | (sub >= i)
    code = jnp.where(keep, 0.0, NEG_INF)
    for n in range(8):
        code_ref[n] = code[n:n + 1, :]

    def bias_chunk(j, c, s, diag):
        h = HEAD_PERM[c // 2]
        cc = c % 2
        bias = bias_ref[h, 0 if diag else jnp.minimum(i - j, 2), :, cc * LANES:(cc + 1) * LANES]
        return s + bias + code_ref[j, :, c * LANES:(c + 1) * LANES]

    l = _flash_t(i, qt, lambda off: kb_ref[pl.ds(off, TQ), :], lambda off: vt_ref[:, pl.ds(off, TQ)],
                 bias_chunk, st_ref, pt_ref, acc_ref)
    o = acc_ref[...] * (1.0 / l)
    for s in range(2):
        o_ref[0, :, s * LANES:(s + 1) * LANES] = _to_rows(_pair_halves(o, s))


def _moba_prompt(q_t, qf_t, k, v_t, bias_t):
    b, _, t = q_t.shape
    assert t // TQ <= 8
    return pl.pallas_call(
        _moba_prompt_kernel,
        grid=(b, t // TQ),
        in_specs=[pl.BlockSpec((1, 256, TQ), lambda bi, i: (bi, 0, i)),
                  pl.BlockSpec((1, 256, TQ), lambda bi, i: (bi, 0, i)),
                  pl.BlockSpec((1, t, LANES), lambda bi, i: (bi, 0, 0)),
                  pl.BlockSpec((1, LANES, t), lambda bi, i: (bi, 0, 0)),
                  pl.BlockSpec(bias_t.shape, lambda bi, i: (0, 0, 0, 0))],
        out_specs=pl.BlockSpec((1, TQ, 256), lambda bi, i: (bi, i, 0)),
        out_shape=jax.ShapeDtypeStruct((b, t, 256), BF16),
        scratch_shapes=[pltpu.VMEM((t, LANES), BF16), pltpu.VMEM((LANES, t), BF16), pltpu.VMEM((8, LANES), F32),
                        pltpu.VMEM((8, 1, 4 * TQ), F32)] + _T_SCRATCH(),
        compiler_params=_cparams(("arbitrary", "arbitrary")),
        name="moba_prompt",
    )(q_t, qf_t, k, v_t, bias_t)


def _diff_prompt_kernel(q_ref, k_ref, v_ref, bias_ref, cl_ref, g2c_ref, o_ref, vt_ref, st_ref, pt_ref, acc_ref,
                        *, lam_init):
    i = pl.program_id(1)

    @pl.when(i == 0)
    def _():
        vt_ref[...] = v_ref[0].astype(BF16)

    lam = _diff_lambda(cl_ref[...], lam_init)
    sub = lax.broadcasted_iota(jnp.int32, (LANES, TQ), 0)
    lo = sub < LANES // 2
    for sl in range(2):
        rows = slice(sl * LANES, (sl + 1) * LANES)
        qt = _blockdiag_cols(q_ref[0, rows, :], 4)

        def bias_chunk(j, c, s, diag, sl=sl):
            cc = c % 2
            dd = 0 if diag else jnp.minimum(i - j, 2)
            return s + bias_ref[2 * sl + c // 4, dd, :, cc * LANES:(cc + 1) * LANES]

        l = _flash_t(i, qt, lambda off, rows=rows: k_ref[0, pl.ds(off, TQ), rows],
                     lambda off, rows=rows: vt_ref[rows, pl.ds(off, TQ)], bias_chunk, st_ref, pt_ref, acc_ref,
                     scale=C_QK_DIM ** -0.5)
        o = acc_ref[...] * (1.0 / l)
        o_h0 = o[:, :TQ] - lam * o[:, TQ:2 * TQ]
        o_h1 = o[:, 2 * TQ:3 * TQ] - lam * o[:, 3 * TQ:]
        pair = jnp.where(lo, o_h0, o_h1)
        sq = pair * pair
        ss_lo = jnp.sum(jnp.where(lo, sq, 0.0), axis=0, keepdims=True)
        ss_hi = jnp.sum(jnp.where(lo, 0.0, sq), axis=0, keepdims=True)
        ms = jnp.where(lo, ss_lo, ss_hi) * (1.0 / D_V)
        y = (pair * lax.rsqrt(ms + EPS) * g2c_ref[...]) * (1.0 - lam_init)
        o_ref[0, :, rows] = _to_rows(y)


def _diff_prompt(q_t, k, v_t, bias_t, cl, g2c, lam_init):
    b, _, t = q_t.shape
    return pl.pallas_call(
        functools.partial(_diff_prompt_kernel, lam_init=lam_init),
        grid=(b, t // TQ),
        in_specs=[pl.BlockSpec((1, 256, TQ), lambda bi, i: (bi, 0, i)),
                  pl.BlockSpec((1, t, 256), lambda bi, i: (bi, 0, 0)),
                  pl.BlockSpec((1, 256, t), lambda bi, i: (bi, 0, 0)),
                  pl.BlockSpec(bias_t.shape, lambda bi, i: (0, 0, 0, 0)),
                  pl.BlockSpec(cl.shape, lambda bi, i: (0, 0)),
                  pl.BlockSpec(g2c.shape, lambda bi, i: (0, 0))],
        out_specs=pl.BlockSpec((1, TQ, 256), lambda bi, i: (bi, i, 0)),
        out_shape=jax.ShapeDtypeStruct((b, t, 256), BF16),
        scratch_shapes=[pltpu.VMEM((256, t), BF16)] + _T_SCRATCH(),
        compiler_params=_cparams(("arbitrary", "arbitrary")),
        name="diff_prompt",
    )(q_t, k, v_t, bias_t, cl, g2c)


def _mla_prompt_kernel(q_ref, ckv_ref, kr_ref, vt_ref, mask_ref, wuvt_ref, o_ref, kc_ref, st_ref, pt_ref, acc_ref):
    i = pl.program_id(1)

    @pl.when(i == 0)
    def _():
        kc_ref[:, :LANES] = ckv_ref[0].astype(BF16)
        kc_ref[:, LANES:] = kr_ref[0].astype(BF16)

    qt = jnp.concatenate([q_ref[0, h * 256:(h + 1) * 256, :] for h in range(N_HEADS)], axis=1)

    def bias_chunk(j, c, s, diag):
        cc = c % 2
        return s + mask_ref[0, :, cc * LANES:(cc + 1) * LANES] if diag else s

    l = _flash_t(i, qt, lambda off: kc_ref[pl.ds(off, TQ), :], lambda off: vt_ref[0, :, pl.ds(off, TQ)],
                 bias_chunk, st_ref, pt_ref, acc_ref, scale=(D_NOPE + D_ROPE) ** -0.5)
    o = (acc_ref[...] * (1.0 / l)).astype(BF16)
    out = _dot(wuvt_ref[0], o[:, :TQ])
    for h in range(1, N_HEADS):
        out = out + _dot(wuvt_ref[h], o[:, h * TQ:(h + 1) * TQ])
    o_ref[0] = _to_rows(out)


def _mla_prompt(qd_t, ckv, kr, ckv_t, mask_t, wuv_t):
    b, _, t = qd_t.shape
    return pl.pallas_call(
        _mla_prompt_kernel,
        grid=(b, t // TQ),
        in_specs=[pl.BlockSpec((1, 1024, TQ), lambda bi, i: (bi, 0, i)),
                  pl.BlockSpec((1, t, LANES), lambda bi, i: (bi, 0, 0)),
                  pl.BlockSpec((1, t, LANES), lambda bi, i: (bi, 0, 0)),
                  pl.BlockSpec((1, LANES, t), lambda bi, i: (bi, 0, 0)),
                  pl.BlockSpec(mask_t.shape, lambda bi, i: (0, 0, 0)),
                  pl.BlockSpec(wuv_t.shape, lambda bi, i: (0, 0, 0))],
        out_specs=pl.BlockSpec((1, TQ, 256), lambda bi, i: (bi, i, 0)),
        out_shape=jax.ShapeDtypeStruct((b, t, 256), BF16),
        scratch_shapes=[pltpu.VMEM((t, 2 * LANES), BF16)] + _T_SCRATCH(),
        compiler_params=_cparams(("arbitrary", "arbitrary")),
        name="mla_prompt",
    )(qd_t, ckv, kr, ckv_t, mask_t, wuv_t)


def _page_copy(hbm, buf, sems, layer, page, slot, j, a):
    return pltpu.make_async_copy(hbm.at[layer, page], buf.at[slot, j], sems.at[slot, a])


def _paged_walk(pt_ref, layer, n_seq, n_chunks, reverse, caches, sems, new_rows, past_chunk, finish):
    P = PAGES_PER_STEP
    total = n_seq * n_chunks

    def issue(step, slot):
        b = step // n_chunks
        c = step % n_chunks
        cl = (n_chunks - 1 - c) if reverse else c
        for j in range(P):
            page = pt_ref[b, cl * P + j]
            for a, (hbm, buf) in enumerate(caches):
                _page_copy(hbm, buf, sems, layer, page, slot, j, a).start()

    def wait(slot):
        for j in range(P):
            for a, (hbm, buf) in enumerate(caches):
                _page_copy(hbm, buf, sems, layer, 0, slot, j, a).wait()

    issue(0, 0)

    def body(step, carry):
        slot = step % 2

        @pl.when(step + 1 < total)
        def _():
            issue(step + 1, 1 - slot)

        wait(slot)
        b = step // n_chunks
        c = step % n_chunks

        @pl.when(c == 0)
        def _():
            new_rows(b)

        past_chunk(b, c, slot)

        @pl.when(c == n_chunks - 1)
        def _():
            finish(b)

        return carry

    lax.fori_loop(0, total, body, 0)


def _page_refs(buf, slot):
    return [buf.at[slot, j] for j in range(PAGES_PER_STEP)]


def _decode_call(kernel_fn, name, page_table, small, caches, bufs, state, out_shape):
    vspec = lambda a: pl.BlockSpec(a.shape, lambda i, pt, n=a.ndim: (0,) * n)
    return pl.pallas_call(
        kernel_fn,
        grid_spec=pltpu.PrefetchScalarGridSpec(
            num_scalar_prefetch=1, grid=(1,),
            in_specs=[vspec(a) for a in small] + [pl.BlockSpec(memory_space=pl.ANY)] * len(caches),
            out_specs=pl.BlockSpec(out_shape.shape, lambda i, pt, n=len(out_shape.shape): (0,) * n),
            scratch_shapes=bufs + [pltpu.SemaphoreType.DMA((2, len(caches)))] + state),
        out_shape=out_shape,
        compiler_params=_cparams(("arbitrary",)),
        name=name,
    )(page_table, *small, *caches)


def _cat_t(refs, sl=None):
    parts = []
    for r in refs:
        x = r[...] if sl is None else r[sl]
        parts.append(x.reshape(-1, x.shape[-1]))
    return jnp.concatenate(parts, axis=1)


def _state_update(m_ref, l_ref, acc_ref, s_blocks, v, first, v_is_t=False):
    rows = acc_ref.shape[0]
    if first:
        m, l, acc = _init_state(rows)
    else:
        m, l, acc = m_ref[...], l_ref[...], acc_ref[...]
    m, l, acc = _softmax_step(s_blocks, m, l, acc, v, v_is_t)
    m_ref[...] = m
    l_ref[...] = l
    acc_ref[...] = acc


def _blockdiag_rows_host(q, n_groups):
    lane = jnp.arange(LANES)
    gw = LANES // n_groups
    parts = [jnp.where((lane >= g * gw) & (lane < (g + 1) * gw), q, jnp.zeros_like(q)) for g in range(n_groups)]
    return jnp.concatenate(parts, axis=1)


def _fox_decode_kernel(pt_ref, q_ref, kn_ref, vn_ref, lfn_ref, own_ref, ck_ref, cv_ref, clf_ref, o_ref,
                       kbuf, vbuf, lfbuf, sems, m_ref, l_ref, acc_ref, cq_ref, s_ref, *, layer, n_chunks):
    P = PAGES_PER_STEP
    nq = own_ref.shape[0]

    def new_rows(b):
        q = q_ref[b]
        x = lfn_ref[b]
        lane = lax.broadcasted_iota(jnp.int32, x.shape, 1)
        s = 1
        while s < nq:
            x = x + jnp.where(lane >= s, pltpu.roll(x, s, 1), 0.0)
            s *= 2
        eye = lax.broadcasted_iota(jnp.int32, (nq, LANES), 0) == lax.broadcasted_iota(jnp.int32, (nq, LANES), 1)
        cols = []
        for h in HEAD_PERM:
            cols.append(jnp.sum(jnp.where(eye, x[h:h + 1, :], 0.0), axis=-1, keepdims=True))
        cq = jnp.concatenate(cols, axis=0)
        cq_ref[...] = cq
        s_ref[...] = jnp.zeros(s_ref.shape, F32)
        sc = _dot_nt(q, kn_ref[b].astype(BF16))
        blocks = []
        for r, h in enumerate(HEAD_PERM):
            blocks.append(sc[r * nq:(r + 1) * nq] + (cq[r * nq:(r + 1) * nq] - x[h:h + 1, :]) + own_ref[...])
        _state_update(m_ref, l_ref, acc_ref, blocks, vn_ref[b].astype(BF16), first=True)

    def past_chunk(b, c, slot):
        q = q_ref[b]
        k_refs, v_refs, lf_refs = _page_refs(kbuf, slot), _page_refs(vbuf, slot), _page_refs(lfbuf, slot)
        x = jnp.concatenate([r[...] for r in lf_refs], axis=1)
        w = x.shape[1]
        lane = lax.broadcasted_iota(jnp.int32, x.shape, 1) % LANES
        incl = x
        s = 1
        while s < LANES:
            incl = incl + jnp.where(lane < LANES - s, pltpu.roll(incl, w - s, 1), 0.0)
            s *= 2
        excl = incl - x
        carry = s_ref[...]
        decay = [None] * P
        for j in range(P - 1, -1, -1):
            decay[j] = excl[:, j * LANES:(j + 1) * LANES] + carry
            carry = carry + incl[:, j * LANES:j * LANES + 1]
        s_ref[...] = carry
        d_all = jnp.concatenate(decay, axis=1)
        sc = _dot(q, _cat_t(k_refs).astype(BF16))
        cq = cq_ref[...]
        blocks = [sc[r * nq:(r + 1) * nq] + (cq[r * nq:(r + 1) * nq] + d_all[h:h + 1, :])
                  for r, h in enumerate(HEAD_PERM)]
        _state_update(m_ref, l_ref, acc_ref, blocks, _cat_t(v_refs).astype(BF16), first=False, v_is_t=True)

    def finish(b):
        o_ref[b] = acc_ref[...] / l_ref[...]

    _paged_walk(pt_ref, layer, q_ref.shape[0], n_chunks, True,
                [(ck_ref, kbuf), (cv_ref, vbuf), (clf_ref, lfbuf)], sems, new_rows, past_chunk, finish)


def _fox_decode(layer, page_table, q, kn, vn, lfn, own, cache_k, cache_v, cache_lf):
    b, rows, _ = q.shape
    P = PAGES_PER_STEP
    n_chunks = page_table.shape[1] // P
    return _decode_call(
        functools.partial(_fox_decode_kernel, layer=layer, n_chunks=n_chunks), "fox_decode", page_table,
        [q, kn, vn, lfn, own], [cache_k, cache_v, cache_lf],
        [pltpu.VMEM((2, P, KV_HEADS, HEAD_DIM, LANES), F32), pltpu.VMEM((2, P, KV_HEADS, HEAD_DIM, LANES), F32),
         pltpu.VMEM((2, P, N_HEADS, LANES), F32)],
        [pltpu.VMEM((rows, 1), F32), pltpu.VMEM((rows, 1), F32), pltpu.VMEM((rows, LANES), F32),
         pltpu.VMEM((rows, 1), F32), pltpu.VMEM((N_HEADS, LANES), F32)],
        jax.ShapeDtypeStruct((b, rows, LANES), F32))


def _moba_decode_kernel(pt_ref, q_ref, qf_ref, bias_ref, own_ref, kn_ref, vn_ref, ck_ref, cv_ref, o_ref,
                        kbuf, vbuf, sems, km_ref, mx_ref, lx_ref, acc_ref, *, layer, n_chunks):
    P = PAGES_PER_STEP
    nq = bias_ref.shape[2]
    rows = 4 * nq
    nbc = P // 2
    nb = nbc * n_chunks
    lane_f = lax.broadcasted_iota(jnp.int32, (LANES, LANES), 1)
    lane_r = lax.broadcasted_iota(jnp.int32, (rows, LANES), 1)

    def new_rows(b):
        km_ref[...] = jnp.zeros(km_ref.shape, F32)
        mx_ref[...] = jnp.full(mx_ref.shape, -jnp.inf, F32)
        lx_ref[...] = jnp.zeros(lx_ref.shape, F32)

    def past_chunk(b, c, slot):
        q = q_ref[b]
        k_refs, v_refs = _page_refs(kbuf, slot), _page_refs(vbuf, slot)
        last = jnp.where(c == n_chunks - 1, 1, 0)
        km, mx, lx = km_ref[...], mx_ref[...], lx_ref[...]
        for n in range(nbc):
            blk = c * nbc + n
            kt = _cat_t(k_refs[2 * n:2 * n + 2])
            vt = _cat_t(v_refs[2 * n:2 * n + 2])
            km = jnp.where(lane_f == blk, jnp.sum(kt, axis=1, keepdims=True) * (1.0 / TQ), km)
            s = _dot(q, kt.astype(BF16))
            dd = last if n == nbc - 1 else 0
            blocks = [s[r * nq:(r + 1) * nq] + bias_ref[h, dd] for r, h in enumerate(HEAD_PERM)]
            m, l, acc = _softmax_step(blocks, *_init_state(rows), vt.astype(BF16), v_is_t=True)
            mx = jnp.where(lane_r == blk, m, mx)
            lx = jnp.where(lane_r == blk, l, lx)
            acc_ref[blk] = acc
        km_ref[...] = km
        mx_ref[...] = mx
        lx_ref[...] = lx

    def finish(b):
        q = q_ref[b]
        g = _dot(qf_ref[b], km_ref[...], precision=lax.Precision.HIGHEST)
        rank = jnp.zeros(g.shape, F32)
        for mth in range(nb):
            col = g[:, mth:mth + 1]
            beats = (col > g) | ((col == g) & (lane_r > mth))
            rank = rank + jnp.where(beats, 1.0, 0.0)
        sel = (rank < float(MOBA_TOPK)) & (lane_r < nb)
        s = _dot_nt(q, kn_ref[b].astype(BF16))
        blocks = [s[r * nq:(r + 1) * nq] + own_ref[h] for r, h in enumerate(HEAD_PERM)]
        m_o, l_o, acc_o = _softmax_step(blocks, *_init_state(rows), vn_ref[b].astype(BF16))
        mx = mx_ref[...]
        m_all = jnp.maximum(jnp.max(jnp.where(sel, mx, -jnp.inf), axis=-1, keepdims=True), m_o)
        w = jnp.where(sel, jnp.exp(mx - m_all), 0.0)
        w_o = jnp.exp(m_o - m_all)
        den = jnp.sum(w * lx_ref[...], axis=-1, keepdims=True) + w_o * l_o
        num = w_o * acc_o
        for n in range(nb):
            num = num + w[:, n:n + 1] * acc_ref[n]
        o_ref[b] = num / den

    _paged_walk(pt_ref, layer, q_ref.shape[0], n_chunks, False, [(ck_ref, kbuf), (cv_ref, vbuf)], sems,
                new_rows, past_chunk, finish)


def _moba_decode(layer, page_table, q, qf, bias, own, kn, vn, cache_k, cache_v):
    b, rows, _ = q.shape
    P = PAGES_PER_STEP
    n_chunks = page_table.shape[1] // P
    nb = n_chunks * P // 2
    assert nb <= LANES
    buf = pltpu.VMEM((2, P, KV_HEADS, HEAD_DIM, LANES), F32)
    return _decode_call(
        functools.partial(_moba_decode_kernel, layer=layer, n_chunks=n_chunks), "moba_decode", page_table,
        [q, qf, bias, own, kn, vn], [cache_k, cache_v], [buf, buf],
        [pltpu.VMEM((LANES, LANES), F32), pltpu.VMEM((rows, LANES), F32), pltpu.VMEM((rows, LANES), F32),
         pltpu.VMEM((nb, rows, LANES), F32)],
        jax.ShapeDtypeStruct((b, rows, LANES), F32))


def _diff_finish(l, acc, lam, g2, lam_init, t):
    o = acc / l
    o_h0 = o[:t] - lam * o[t:2 * t]
    o_h1 = o[2 * t:3 * t] - lam * o[3 * t:]
    pair = jnp.where(_lane_lo(o_h0.shape), o_h0, o_h1)
    return _subln(pair, g2, lam_init)


def _diff_decode_kernel(pt_ref, q_ref, kn_ref, vn_ref, own_ref, bias_ref, cl_ref, g2_ref, ck_ref, cv_ref, o_ref,
                        kbuf, vbuf, sems, m_ref, l_ref, acc_ref, *, layer, n_chunks, lam_init):
    P = PAGES_PER_STEP
    nq = own_ref.shape[1]
    scale = C_QK_DIM ** -0.5
    heads_of = lambda sl: [2 * sl + r // 2 for r in range(4)]

    def new_rows(b):
        for sl in range(2):
            lanes = slice(sl * LANES, (sl + 1) * LANES)
            s = _dot_nt(q_ref[b, sl], kn_ref[b, :, lanes].astype(BF16)) * scale
            blocks = [s[r * nq:(r + 1) * nq] + own_ref[h] for r, h in enumerate(heads_of(sl))]
            _state_update(m_ref.at[sl], l_ref.at[sl], acc_ref.at[sl], blocks,
                          vn_ref[b, :, lanes].astype(BF16), first=True)

    def past_chunk(b, c, slot):
        k_refs, v_refs = _page_refs(kbuf, slot), _page_refs(vbuf, slot)
        last = jnp.where(c == n_chunks - 1, 1, 0)
        for sl in range(2):
            hs = slice(2 * sl, 2 * sl + 2)
            s = _dot(q_ref[b, sl], _cat_t(k_refs, hs).astype(BF16)) * scale
            blocks = []
            for r, h in enumerate(heads_of(sl)):
                bias = jnp.concatenate([bias_ref[h, 0]] * (P - 1) + [bias_ref[h, last]], axis=1)
                blocks.append(s[r * nq:(r + 1) * nq] + bias)
            _state_update(m_ref.at[sl], l_ref.at[sl], acc_ref.at[sl], blocks,
                          _cat_t(v_refs, hs).astype(BF16), first=False, v_is_t=True)

    def finish(b):
        lam = _diff_lambda(cl_ref[...], lam_init)
        for sl in range(2):
            o_ref[b, :, sl * LANES:(sl + 1) * LANES] = _diff_finish(
                l_ref[sl], acc_ref[sl], lam, g2_ref[...], lam_init, nq).astype(BF16)

    _paged_walk(pt_ref, layer, q_ref.shape[0], n_chunks, False, [(ck_ref, kbuf), (cv_ref, vbuf)], sems,
                new_rows, past_chunk, finish)


def _diff_decode(layer, page_table, q, kn, vn, own, bias, cl, g2, cache_k, cache_v, lam_init):
    b, _, rows, _ = q.shape
    nq = rows // 4
    P = PAGES_PER_STEP
    n_chunks = page_table.shape[1] // P
    buf = pltpu.VMEM((2, P, N_HEADS, HEAD_DIM, LANES), F32)
    return _decode_call(
        functools.partial(_diff_decode_kernel, layer=layer, n_chunks=n_chunks, lam_init=lam_init), "diff_decode",
        page_table, [q, kn, vn, own, bias, cl, g2], [cache_k, cache_v], [buf, buf],
        [pltpu.VMEM((2, rows, 1), F32), pltpu.VMEM((2, rows, 1), F32), pltpu.VMEM((2, rows, LANES), F32)],
        jax.ShapeDtypeStruct((b, nq, 256), BF16))


def _mla_out(o, wuv_ref, t):
    out = _dot(o[:t].astype(BF16), wuv_ref[0])
    for h in range(1, N_HEADS):
        out = out + _dot(o[h * t:(h + 1) * t].astype(BF16), wuv_ref[h])
    return out


def _mla_decode_kernel(pt_ref, ql_ref, qr_ref, cn_ref, rn_ref, own_ref, wuv_ref, cc_ref, cr_ref, o_ref,
                       cbuf, rbuf, sems, m_ref, l_ref, acc_ref, *, layer, n_chunks):
    nq = own_ref.shape[0]
    scale = (D_NOPE + D_ROPE) ** -0.5

    def new_rows(b):
        ckv = cn_ref[b].astype(BF16)
        s = (_dot_nt(ql_ref[b], ckv) + _dot_nt(qr_ref[b], rn_ref[b].astype(BF16))) * scale
        blocks = [s[r * nq:(r + 1) * nq] + own_ref[...] for r in range(N_HEADS)]
        _state_update(m_ref, l_ref, acc_ref, blocks, ckv, first=True)

    def past_chunk(b, c, slot):
        ckv = jnp.concatenate([r[...] for r in _page_refs(cbuf, slot)], axis=0).astype(BF16)
        krt = jnp.concatenate([r[...] for r in _page_refs(rbuf, slot)], axis=1).astype(BF16)
        s = (_dot_nt(ql_ref[b], ckv) + _dot(qr_ref[b], krt)) * scale
        blocks = [s[r * nq:(r + 1) * nq] for r in range(N_HEADS)]
        _state_update(m_ref, l_ref, acc_ref, blocks, ckv, first=False)

    def finish(b):
        o_ref[b] = _mla_out(acc_ref[...] / l_ref[...], wuv_ref, nq).astype(BF16)

    _paged_walk(pt_ref, layer, ql_ref.shape[0], n_chunks, False, [(cc_ref, cbuf), (cr_ref, rbuf)], sems,
                new_rows, past_chunk, finish)


def _mla_decode(layer, page_table, ql, qr, cn, rn, own, wuv, cache_ckv, cache_kr):
    b, rows, _ = ql.shape
    nq = rows // N_HEADS
    P = PAGES_PER_STEP
    n_chunks = page_table.shape[1] // P
    return _decode_call(
        functools.partial(_mla_decode_kernel, layer=layer, n_chunks=n_chunks), "mla_decode", page_table,
        [ql, qr, cn, rn, own, wuv], [cache_ckv, cache_kr],
        [pltpu.VMEM((2, P, LANES, LANES), F32), pltpu.VMEM((2, P, D_ROPE, LANES), F32)],
        [pltpu.VMEM((rows, 1), F32), pltpu.VMEM((rows, 1), F32), pltpu.VMEM((rows, LANES), F32)],
        jax.ShapeDtypeStruct((b, nq, 256), BF16))


def _bucket_table():
    d = np.arange(MAX_DISTANCE + 1)
    max_exact = N_BUCKETS // 2
    df = np.maximum(d, 1).astype(np.float32)
    large = max_exact + (np.log(df / max_exact) / math.log(MAX_DISTANCE / max_exact)
                         * (N_BUCKETS - max_exact)).astype(np.int32)
    large = np.minimum(large, N_BUCKETS - 1)
    return np.where(d < max_exact, d, large)


def _tables(rel_bias, t, nq, q0, bs):
    def select(tab, idx):
        onehot = np.zeros((tab.shape[1], idx.size), np.float32)
        onehot[idx, np.arange(idx.size)] = 1.0
        return jnp.dot(tab, onehot, precision=lax.Precision.HIGHEST)

    def toeplitz_t(line):
        return jnp.stack([line[:, TQ - 1 - c:2 * TQ - 1 - c] for c in range(TQ)], axis=1)

    bd = select(rel_bias.T, _bucket_table())
    r = np.arange(TQ)[None, :]
    c = np.arange(TQ)[:, None]
    k = np.arange(2 * TQ - 1) - (TQ - 1)
    tile0 = toeplitz_t(select(bd, np.clip(k, 0, MAX_DISTANCE)))
    tile1 = toeplitz_t(select(bd, np.clip(TQ + k, 0, MAX_DISTANCE)))
    far = jnp.broadcast_to(bd[:, MAX_DISTANCE][:, None, None], (bd.shape[0], TQ, TQ))
    prompt_t = jnp.stack([jnp.where(r >= c, tile0, NEG_INF), tile1, far], axis=1)
    mask_t = jnp.stack([jnp.where(r >= c, 0.0, NEG_INF).astype(F32), jnp.zeros((TQ, TQ), F32)])
    tq = np.arange(nq)[:, None]
    u = np.arange(LANES)[None, :]
    own_ok = (u <= tq) & (u < nq)
    own = select(bd, np.clip(tq - u, 0, MAX_DISTANCE).ravel()).reshape(-1, nq, LANES)
    own = jnp.where(own_ok, own, NEG_INF)
    own_mask = jnp.where(own_ok, 0.0, NEG_INF).astype(F32)
    idx_last = np.clip(LANES + tq - u, 0, MAX_DISTANCE)
    page_last = select(bd, idx_last.ravel()).reshape(-1, nq, LANES)
    page_far = jnp.broadcast_to(bd[:, MAX_DISTANCE][:, None, None], (bd.shape[0], nq, LANES))
    page = jnp.stack([page_far, page_last], axis=1)
    blk = jnp.stack([jnp.concatenate([page_far, page_far], axis=-1),
                     jnp.concatenate([page_far, page_last], axis=-1)], axis=1)
    cos_p, sin_p = _rope_tables(jnp.arange(t))
    cos_s, sin_s = _rope_tables(q0 + jnp.arange(nq))
    return dict(prompt_t=prompt_t, mask_t=mask_t,
                own=own, own_mask=own_mask, page=page, blk=blk,
                cos_p=cos_p, sin_p=sin_p, cos_pt=cos_p.T, sin_pt=sin_p.T,
                cos_s=jnp.tile(cos_s, (bs, 1)), sin_s=jnp.tile(sin_s, (bs, 1)))


def _rope_tables(pos):
    half = D_ROPE // 2
    inv = jnp.power(ROPE_BASE, -jnp.arange(half, dtype=F32) / half)
    ang = pos.astype(F32)[:, None] * inv
    c, s = jnp.cos(ang), jnp.sin(ang)
    pad = jnp.zeros((pos.shape[0], LANES - D_ROPE), F32)
    return jnp.concatenate([c, c, pad], axis=1), jnp.concatenate([-s, s, pad], axis=1)


def _perm_heads(w, axis):
    parts = jnp.split(w, N_HEADS, axis=axis)
    return jnp.concatenate([parts[h] for h in HEAD_PERM], axis=axis)


def _layer_weights(l, norm_g, w_in, b_forget, b_gate, d_q_norm_g, d_w_q_up, d_kv_norm_g, d_w_kv_up,
                   c_subln_g, w_branch, w_out):
    w = w_in[l]
    d = w.shape[0]
    splits = (256, 128, 128, 4, 256, 256, 128, 128, 256, 256, 256, 256, 256, 256, 128, 32, 256, 4 * d)
    offs = np.cumsum((0,) + splits)
    (a_q, a_k, a_v, a_f, a_z, b_q, b_k, b_v, b_z, c_q, c_k, c_v, c_z, d_qa, d_kva, d_kr, d_z, gates) = [
        w[:, offs[i]:offs[i + 1]] for i in range(len(splits))]
    scale = HEAD_DIM ** -0.5
    zpad = lambda x, n: jnp.concatenate([x, jnp.zeros((d, n - x.shape[1]), x.dtype)], axis=1)
    swap = jnp.concatenate([d_kr[:, D_ROPE // 2:], d_kr[:, :D_ROPE // 2]], axis=1)
    segs = dict(qa=_perm_heads(a_q, 1) * scale, ka=a_k, va=a_v, qb=_perm_heads(b_q, 1) * scale, kb=b_k, vb=b_v,
                qc=c_q, kc=c_k, vc=c_v, dqa=d_qa, dkva=d_kva, kr=zpad(d_kr, LANES), krs=zpad(swap, LANES),
                z=jnp.concatenate([_perm_heads(a_z, 1), _perm_heads(b_z, 1), c_z, d_z], axis=1))
    w_pack = jnp.concatenate([zpad(a_f, LANES) if n == "af" else segs[n] for n, _ in _SEGS], axis=1).astype(BF16)
    w_rows = jnp.concatenate([segs[n] for n, _ in _SEGS_R], axis=1).astype(BF16)
    w_t = jnp.concatenate([zpad(a_f, 8) if n == "af" else segs[n] for n, _ in _SEGS_T], axis=1).astype(BF16).T
    wq = d_w_q_up[l]
    r = wq.shape[0]
    nope = wq[:, :, :D_NOPE].reshape(r, N_HEADS * D_NOPE)
    rope = wq[:, :, D_NOPE:]
    rope_sw = jnp.concatenate([rope[..., D_ROPE // 2:], rope[..., :D_ROPE // 2]], axis=-1)

    def spread(x):
        z1 = jnp.zeros((r, N_HEADS, D_LAT), x.dtype)
        z2 = jnp.zeros((r, N_HEADS, 256 - D_LAT - D_ROPE), x.dtype)
        return jnp.concatenate([z1, x, z2], axis=-1).reshape(r, N_HEADS * 256)

    wq_pack = jnp.concatenate([nope, spread(rope), spread(rope_sw)], axis=1).astype(BF16)
    wkv = d_w_kv_up[l]
    w_uk = wkv[:, :, :D_NOPE]
    w_uv = wkv[:, :, D_NOPE:]
    wuk = jnp.zeros((N_HEADS, D_NOPE, N_HEADS, 256), F32)
    wuv = jnp.zeros((N_HEADS, D_LAT, N_HEADS, D_V), F32)
    for h in range(N_HEADS):
        wuk = wuk.at[h, :, h, :D_LAT].set(w_uk[:, h, :].T)
        wuv = wuv.at[h, :, h, :].set(w_uv[:, h, :])
    wuk = wuk.reshape(N_HEADS * D_NOPE, N_HEADS * 256).astype(BF16)
    wuv = wuv.reshape(N_HEADS, D_LAT, N_HEADS * D_V).astype(BF16)
    wbr = w_branch[l]
    wbr = jnp.stack([_perm_heads(wbr[0], 0), _perm_heads(wbr[1], 0), wbr[2], wbr[3]])
    bf = jnp.concatenate([b_forget[l].astype(F32), jnp.zeros((LANES - N_HEADS,), F32)])
    g2 = jnp.concatenate([c_subln_g[l], c_subln_g[l]])
    return dict(
        norm_g=norm_g[l][None, :], w_pack=w_pack, w_rows=w_rows, w_t=w_t,
        b_forget=bf[None, :], bf_col=bf[:8, None],
        gq=d_q_norm_g[l][None, :], gq_col=d_q_norm_g[l][:, None],
        gkv=d_kv_norm_g[l][None, :], gkv_col=d_kv_norm_g[l][:, None],
        wq=wq_pack, wq_t=wq_pack.T, wuk=wuk, wuk_t=wuk.T, wuv=wuv, wuv_t=jnp.swapaxes(wuv, 1, 2),
        wg=gates.astype(BF16), b_gate=b_gate[l][None, :], wbr=wbr.astype(BF16), wout=w_out[l].astype(BF16),
        g2=g2[None, :], g2_col=g2[:, None])


def _unbd(o, nq):
    lo = jnp.arange(LANES) < LANES // 2
    left = jnp.where(lo, o[:, :nq], o[:, nq:2 * nq])
    right = jnp.where(lo, o[:, 2 * nq:3 * nq], o[:, 3 * nq:])
    return jnp.concatenate([left, right], axis=-1).reshape(-1, 256)


def _pad_page(x):
    return jnp.pad(x, ((0, 0), (0, LANES - x.shape[1]), (0, 0)))


def kernel(x_prompt, x_sample, cache_a_k, cache_a_v, cache_a_logf, cache_b_k, cache_b_v, cache_c_k, cache_c_v, cache_d_ckv, cache_d_kr, page_table, norm_g, w_in, b_forget, b_gate, d_q_norm_g, d_w_q_up, d_kv_norm_g, d_w_kv_up, c_lambda, c_subln_g, w_branch, w_out, rel_bias, final_norm_g):
    bp, t, d = x_prompt.shape
    bs, nq, _ = x_sample.shape
    depth, n_phys, page = cache_a_k.shape[:3]
    n_pages = page_table.shape[1]
    q0 = n_pages * page
    assert page == LANES and t % TQ == 0 and n_pages % PAGES_PER_STEP == 0 and nq % 8 == 0

    t5 = lambda c: jnp.transpose(c, (0, 1, 3, 4, 2))
    ca_k, ca_v, cb_k, cb_v, cc_k, cc_v = map(t5, (cache_a_k, cache_a_v, cache_b_k, cache_b_v, cache_c_k, cache_c_v))
    ca_lf = jnp.swapaxes(cache_a_logf, 2, 3)
    cd_kr = jnp.swapaxes(cache_d_kr, 2, 3)

    tabs = _tables(rel_bias.astype(F32), t, nq, q0, bs)
    tm_s = bs * nq
    final_g = final_norm_g[None, :]
    rows = 4 * nq

    hp = x_prompt.reshape(bp * t, d)
    hs = x_sample.reshape(bs * nq, d)
    rows_p, rows_s = [], []
    for l in range(depth):
        lam_init = 0.8 - 0.6 * math.exp(-0.3 * l)
        lw = _layer_weights(l, norm_g, w_in, b_forget, b_gate, d_q_norm_g, d_w_q_up, d_kv_norm_g, d_w_kv_up,
                            c_subln_g, w_branch, w_out)
        cl = c_lambda[l].astype(F32)
        last = l == depth - 1

        pr = _inproj_prompt(hp, lw, tabs, bp, t, 256)
        r3 = lambda a: a.reshape(bp, t, a.shape[-1])
        frow = _cumsum_rows(pr["lfT"])
        fkrep = jnp.broadcast_to(frow[:, :N_HEADS, :, None], (bp, N_HEADS, t, LANES))
        o_a = _fox_prompt(pr["qaT"], r3(pr["ka"]), pr["vaT"], frow, fkrep, tabs["mask_t"])
        o_b = _moba_prompt(pr["qbT"], pr["qbfT"], r3(pr["kb"]), pr["vbT"], tabs["prompt_t"][:N_HEADS])
        o_c = _diff_prompt(pr["qcT"], r3(pr["kc"]), pr["vcT"], tabs["prompt_t"][N_HEADS:], cl, lw["g2_col"], lam_init)
        o_d = _mla_prompt(pr["qdT"], r3(pr["ckv"]), r3(pr["kr"]), pr["ckvT"], tabs["mask_t"], lw["wuv_t"])
        f2 = lambda a: a.reshape(bp * t, a.shape[-1])
        hp = _merge(hp, f2(o_a), f2(o_b), f2(o_c), f2(o_d), pr["z"], lw, final_g, last, 512)
        heads_t = lambda a, nh: jnp.transpose(a.reshape(bp, nh, -1, t), (0, 3, 1, 2))
        rows_p.append((heads_t(pr["kaT"], KV_HEADS), heads_t(pr["vaT"], KV_HEADS),
                       jnp.swapaxes(pr["lfT"][:, :N_HEADS], 1, 2),
                       heads_t(pr["kbT"], KV_HEADS), heads_t(pr["vbT"], KV_HEADS),
                       heads_t(pr["kcT"], N_HEADS), heads_t(pr["vcT"], N_HEADS),
                       r3(pr["ckv"]), jnp.swapaxes(pr["krT"][:, :D_ROPE], 1, 2)))

        sr = _inproj(hs, lw, tabs["cos_s"], tabs["sin_s"], tm_s)
        s3 = lambda a: a.reshape(bs, nq, a.shape[-1])
        lf_s = s3(sr["lf"])[:, :, :N_HEADS]
        qa = s3(sr["qa"])
        q_bd = jnp.concatenate([_blockdiag_rows_host(qa[..., :LANES], 2), _blockdiag_rows_host(qa[..., LANES:], 2)], axis=1)
        lfn = jnp.pad(jnp.swapaxes(lf_s, 1, 2), ((0, 0), (0, 8 - N_HEADS), (0, LANES - nq)))
        o = _fox_decode(l, page_table, q_bd, _pad_page(s3(sr["ka"])), _pad_page(s3(sr["va"])), lfn,
                        tabs["own_mask"], ca_k, ca_v, ca_lf)
        o_a = _unbd(o, nq).astype(BF16)
        qb = s3(sr["qb"])
        qbf = s3(sr["qbf"])
        q_bd = jnp.concatenate([_blockdiag_rows_host(qb[..., :LANES], 2), _blockdiag_rows_host(qb[..., LANES:], 2)], axis=1)
        qf_bd = jnp.concatenate([_blockdiag_rows_host(qbf[..., :LANES], 2), _blockdiag_rows_host(qbf[..., LANES:], 2)], axis=1)
        o = _moba_decode(l, page_table, q_bd, qf_bd, tabs["blk"][:N_HEADS], tabs["own"][:N_HEADS],
                         _pad_page(s3(sr["kb"])), _pad_page(s3(sr["vb"])), cb_k, cb_v)
        o_b = _unbd(o, nq).astype(BF16)
        qc = s3(sr["qc"])
        q4 = jnp.stack([_blockdiag_rows_host(qc[..., :LANES], 4), _blockdiag_rows_host(qc[..., LANES:], 4)], axis=1)
        o_c = _diff_decode(l, page_table, q4, _pad_page(s3(sr["kc"])), _pad_page(s3(sr["vc"])),
                           tabs["own"][N_HEADS:], tabs["page"][N_HEADS:], cl, lw["g2"], cc_k, cc_v, lam_init)
        o_c = o_c.reshape(bs * nq, 256)
        qd = jnp.swapaxes(s3(sr["qd"]).reshape(bs, nq, N_HEADS, 256), 1, 2).reshape(bs, rows, 256)
        o_d = _mla_decode(l, page_table, qd[..., :D_LAT], qd[..., D_LAT:D_LAT + D_ROPE],
                          _pad_page(s3(sr["ckv"])), _pad_page(s3(sr["kr"])[..., :D_ROPE]),
                          tabs["own_mask"], lw["wuv"], cache_d_ckv, cd_kr)
        o_d = o_d.reshape(bs * nq, 256)
        hs = _merge(hs, o_a, o_b, o_c, o_d, sr["z"], lw, final_g, last, tm_s)
        rows_s.append((sr["ka"].reshape(bs, nq, KV_HEADS, HEAD_DIM), sr["va"].reshape(bs, nq, KV_HEADS, HEAD_DIM), lf_s,
                       sr["kb"].reshape(bs, nq, KV_HEADS, HEAD_DIM), sr["vb"].reshape(bs, nq, KV_HEADS, HEAD_DIM),
                       sr["kc"].reshape(bs, nq, N_HEADS, 2 * C_QK_DIM), sr["vc"].reshape(bs, nq, N_HEADS, HEAD_DIM),
                       s3(sr["ckv"]), s3(sr["kr"])[:, :, :D_ROPE]))

    stack = lambda rows, i: jnp.stack([r[i] for r in rows], axis=0)
    return ((hp.reshape(bp, t, d), hs.reshape(bs, nq, d))
            + tuple(stack(rows_p, i) for i in range(9)) + tuple(stack(rows_s, i) for i in range(9)))
```
